```python
import math
import jax
import jax.numpy as jnp
from jax import lax
import numpy as np

D_MODEL = 1024
BATCH = 2
SEQ = 8192
DEPTH = 4
DEC_BATCH = 32
DEC_SEQ = 8
PAST_LEN = 8192
PAGE_SIZE = 128

N_MIXERS = 2
N_A_LAYERS = (DEPTH + 1) // 2
N_B_LAYERS = DEPTH // 2
A_HEADS = 4
A_HEAD_DIM = 128
A_V_DIM = 2 * A_HEAD_DIM
A_QK_WIDTH = A_HEADS * 2 * A_HEAD_DIM
A_V_WIDTH = A_HEADS * A_V_DIM
A_IN_WIDTH = 2 * A_QK_WIDTH + 2 * A_V_WIDTH
Q_BLOCK = 128
B_HEADS = 8
B_HEAD_DIM = 128
B_WIDTH = B_HEADS * B_HEAD_DIM
B_CONV_CH = 3 * B_WIDTH
B_IN_WIDTH = 4 * B_WIDTH + 2 * B_HEADS
CONV_WIDTH = 4
GDN_CHUNK = 64
PLE_DIM = 256
NORM_EPS = 1e-6

kernel_name = 'hybrid_diffattn_gdn_decoder_step'


def _rms_norm(x, g=None):
    xf = x.astype(jnp.float32)
    y = xf * lax.rsqrt(jnp.mean(xf * xf, axis=-1, keepdims=True) + NORM_EPS)
    if g is not None:
        y = y * g.astype(jnp.float32)
    return y.astype(x.dtype)


def _l2norm(x):
    xf = x.astype(jnp.float32)
    return xf * lax.rsqrt(jnp.sum(xf * xf, axis=-1, keepdims=True) + NORM_EPS)


def _alibi_slopes():
    return jnp.exp2(-8.0 * jnp.arange(1, A_HEADS + 1, dtype=jnp.float32) / A_HEADS)


def _diff_attn_in(xn, w_in):
    bsz, L, _ = xn.shape
    proj = xn @ w_in
    q = proj[..., :A_QK_WIDTH].reshape(bsz, L, A_HEADS, 2, A_HEAD_DIM)
    k = proj[..., A_QK_WIDTH:2 * A_QK_WIDTH].reshape(bsz, L, A_HEADS, 2, A_HEAD_DIM)
    v = proj[..., 2 * A_QK_WIDTH:2 * A_QK_WIDTH + A_V_WIDTH].reshape(bsz, L, A_HEADS, A_V_DIM)
    gate = proj[..., 2 * A_QK_WIDTH + A_V_WIDTH:]
    return q, k, v, gate


def _diff_scores(q, k, q_pos, k_pos, slopes):
    s = jnp.einsum('bqhcd,bkhcd->bhcqk', q, k, preferred_element_type=jnp.float32) * (A_HEAD_DIM ** -0.5)
    dist = (q_pos[:, None] - k_pos[None, :]).astype(jnp.float32)
    s = s + (-slopes[:, None, None] * dist)[None, :, None]
    return jnp.where(dist >= 0, s, -jnp.inf)


def _diff_weights(s, lam):
    p = jax.nn.softmax(s, axis=-1)
    return p[:, :, 0] - lam * p[:, :, 1]


def _diff_attn_prompt(q, k, v, lam, slopes):
    bsz, L = q.shape[:2]
    nb = L // Q_BLOCK
    pos = jnp.arange(L)
    q_blocks = jnp.moveaxis(q.reshape(bsz, nb, Q_BLOCK, A_HEADS, 2, A_HEAD_DIM), 1, 0)

    def block(args):
        q_blk, q_pos = args
        w = _diff_weights(_diff_scores(q_blk, k, q_pos, pos, slopes), lam)
        return jnp.einsum('bhqk,bkhe->bqhe', w, v)

    o = lax.map(block, (q_blocks, pos.reshape(nb, Q_BLOCK)))
    return jnp.moveaxis(o, 0, 1).reshape(bsz, L, A_HEADS, A_V_DIM)


def _diff_attn_sample(q, k, v, k_past, v_past, q_pos, past_pos, lam, slopes):
    s_past = _diff_scores(q, k_past, q_pos, past_pos, slopes)
    s_new = _diff_scores(q, k, q_pos, q_pos, slopes)
    w = _diff_weights(jnp.concatenate([s_past, s_new], axis=-1), lam)
    n_past = k_past.shape[1]
    return (jnp.einsum('bhqk,bkhe->bqhe', w[..., :n_past], v_past)
            + jnp.einsum('bhqk,bkhe->bqhe', w[..., n_past:], v))


def _diff_attn_out(o, gate, subln, lam_init, w_out, dtype):
    bsz, L = o.shape[:2]
    o = _rms_norm(o, subln) * (1.0 - lam_init)
    o = o * jax.nn.silu(gate.astype(jnp.float32)).reshape(o.shape)
    return o.reshape(bsz, L, A_V_WIDTH).astype(dtype) @ w_out


def _causal_conv(x, buf, w):
    xp = jnp.concatenate([buf.astype(x.dtype), x], axis=1)
    y = lax.conv_general_dilated(xp, w.astype(x.dtype)[:, None, :], window_strides=(1,), padding='VALID',
                                 dimension_numbers=('NWC', 'WIO', 'NWC'), feature_group_count=x.shape[-1])
    return jax.nn.silu(y), xp[:, xp.shape[1] - (CONV_WIDTH - 1):]


def _gated_delta(q, k, v, g, beta, s0):
    bsz, L, H, dk = q.shape
    dv = v.shape[-1]
    C = math.gcd(L, GDN_CHUNK)
    N = L // C
    f32 = jnp.float32

    def chunks(t):
        return t.astype(f32).reshape(bsz, N, C, H, -1).transpose(0, 3, 1, 2, 4)

    q = chunks(q) * (dk ** -0.5)
    k = chunks(k)
    v = chunks(v)
    g = g.astype(f32).reshape(bsz, N, C, H).transpose(0, 3, 1, 2)
    beta = beta.astype(f32).reshape(bsz, N, C, H).transpose(0, 3, 1, 2)
    gc = jnp.cumsum(g, axis=-1)
    tril = jnp.tril(jnp.ones((C, C), dtype=bool))
    strict = jnp.tril(jnp.ones((C, C), dtype=bool), -1)
    diff = gc[..., :, None] - gc[..., None, :]
    decay = jnp.where(tril, jnp.exp(jnp.where(tril, diff, 0.0)), 0.0)
    kb = k * beta[..., None]
    vb = v * beta[..., None]
    a_mat = jnp.eye(C, dtype=f32) + jnp.where(strict, jnp.einsum('bhnid,bhnjd->bhnij', kb, k) * decay, 0.0)
    rhs = jnp.concatenate([vb, kb * jnp.exp(gc)[..., None]], axis=-1)
    sol = lax.linalg.triangular_solve(a_mat, rhs, left_side=True, lower=True, unit_diagonal=True)
    u, w = sol[..., :dv], sol[..., dv:]
    qk = jnp.einsum('bhnid,bhnjd->bhnij', q, k) * decay
    g_last = gc[..., -1]
    q_g = q * jnp.exp(gc)[..., None]
    k_tail = k * jnp.exp(g_last[..., None] - gc)[..., None]

    def step(s, xs):
        u_n, w_n, qk_n, qg_n, kt_n, gl_n = xs
        v_new = u_n - jnp.einsum('bhck,bhkv->bhcv', w_n, s)
        o_n = jnp.einsum('bhck,bhkv->bhcv', qg_n, s) + jnp.einsum('bhij,bhjv->bhiv', qk_n, v_new)
        s = s * jnp.exp(gl_n)[..., None, None] + jnp.einsum('bhck,bhcv->bhkv', kt_n, v_new)
        return s, o_n

    xs = tuple(jnp.moveaxis(t, 2, 0) for t in (u, w, qk, q_g, k_tail, g_last))
    s, o = lax.scan(step, s0.astype(f32), xs)
    o = o.transpose(1, 0, 3, 2, 4).reshape(bsz, L, H, dv)
    return o, s


def _gdn_mixer(xn, conv_buf, s0, w_in, conv_w, a_log, dt_bias, onorm, w_out):
    bsz, L, _ = xn.shape
    f32 = jnp.float32
    proj = xn @ w_in
    qkv = proj[..., :B_CONV_CH]
    z = proj[..., B_CONV_CH:B_CONV_CH + B_WIDTH]
    b = proj[..., B_CONV_CH + B_WIDTH:B_CONV_CH + B_WIDTH + B_HEADS]
    a = proj[..., B_CONV_CH + B_WIDTH + B_HEADS:]
    qkv, new_buf = _causal_conv(qkv, conv_buf, conv_w)
    q = qkv[..., :B_WIDTH].reshape(bsz, L, B_HEADS, B_HEAD_DIM)
    k = qkv[..., B_WIDTH:2 * B_WIDTH].reshape(bsz, L, B_HEADS, B_HEAD_DIM)
    v = qkv[..., 2 * B_WIDTH:].reshape(bsz, L, B_HEADS, B_HEAD_DIM)
    q, k = _l2norm(q), _l2norm(k)
    beta = jax.nn.sigmoid(b.astype(f32))
    g = -jnp.exp(a_log.astype(f32)) * jax.nn.softplus(a.astype(f32) + dt_bias.astype(f32))
    o, s_new = _gated_delta(q, k, v, g, beta, s0)
    o = _rms_norm(o, onorm) * jax.nn.silu(z.astype(f32)).reshape(bsz, L, B_HEADS, B_HEAD_DIM)
    y = o.reshape(bsz, L, B_WIDTH).astype(xn.dtype) @ w_out
    return y, new_buf, s_new


def _ple(h, p, w_ple, w_gate):
    return h + (p @ w_ple) * jax.nn.sigmoid(_rms_norm(h) @ w_gate)


def setup_inputs(seed: int = 0) -> dict:
    key = jax.random.key(seed)
    keys = iter(jax.random.split(key, 32))
    f32 = jnp.float32

    def nrm(shape, scale):
        return jax.random.normal(next(keys), shape, f32) * scale

    n_pages = PAST_LEN // PAGE_SIZE
    n_used = DEC_BATCH * n_pages
    n_phys = n_used + n_used // 4
    x_prompt = nrm((BATCH, SEQ, D_MODEL), 1.0)
    x_sample = nrm((DEC_BATCH, DEC_SEQ, D_MODEL), 1.0)
    cache_k = nrm((N_A_LAYERS, n_phys, PAGE_SIZE, A_HEADS, A_V_DIM), 1.0)
    cache_v = nrm((N_A_LAYERS, n_phys, PAGE_SIZE, A_HEADS, A_V_DIM), 1.0)
    state_conv = nrm((N_B_LAYERS, DEC_BATCH, CONV_WIDTH - 1, B_CONV_CH), 1.0)
    state_ssm = nrm((N_B_LAYERS, DEC_BATCH, B_HEADS, B_HEAD_DIM, B_HEAD_DIM), 0.1)
    page_table = jax.random.permutation(next(keys), n_phys)[:n_used].reshape(DEC_BATCH, n_pages).astype(jnp.int32)
    p_prompt = nrm((DEPTH, BATCH, SEQ, PLE_DIM), 1.0)
    p_sample = nrm((DEPTH, DEC_BATCH, DEC_SEQ, PLE_DIM), 1.0)
    norm_pre = 1.0 + nrm((DEPTH, D_MODEL), 0.05)
    norm_post = 1.0 + nrm((DEPTH, D_MODEL), 0.05)
    w_in_a = nrm((N_A_LAYERS, D_MODEL, A_IN_WIDTH), D_MODEL ** -0.5)
    lambda_q1 = nrm((N_A_LAYERS, A_HEAD_DIM), 0.1)
    lambda_k1 = nrm((N_A_LAYERS, A_HEAD_DIM), 0.1)
    lambda_q2 = nrm((N_A_LAYERS, A_HEAD_DIM), 0.1)
    lambda_k2 = nrm((N_A_LAYERS, A_HEAD_DIM), 0.1)
    subln_a = 1.0 + nrm((N_A_LAYERS, A_V_DIM), 0.05)
    w_out_a = nrm((N_A_LAYERS, A_V_WIDTH, D_MODEL), A_V_WIDTH ** -0.5)
    w_in_b = nrm((N_B_LAYERS, D_MODEL, B_IN_WIDTH), D_MODEL ** -0.5)
    conv_b = nrm((N_B_LAYERS, CONV_WIDTH, B_CONV_CH), CONV_WIDTH ** -0.5)
    a_log_b = jnp.log(jax.random.uniform(next(keys), (N_B_LAYERS, B_HEADS), f32, 1.0, 16.0))
    dt = jnp.exp(jax.random.uniform(next(keys), (N_B_LAYERS, B_HEADS), f32, math.log(1e-3), math.log(1e-1)))
    dt_bias_b = dt + jnp.log(-jnp.expm1(-dt))
    onorm_b = 1.0 + nrm((N_B_LAYERS, B_HEAD_DIM), 0.05)
    w_out_b = nrm((N_B_LAYERS, B_WIDTH, D_MODEL), B_WIDTH ** -0.5)
    w_ple = nrm((DEPTH, PLE_DIM, D_MODEL), PLE_DIM ** -0.5)
    w_ple_gate = nrm((DEPTH, D_MODEL, D_MODEL), D_MODEL ** -0.5)
    return {'x_prompt': x_prompt, 'x_sample': x_sample, 'cache_k': cache_k, 'cache_v': cache_v,
            'state_conv': state_conv, 'state_ssm': state_ssm, 'page_table': page_table,
            'p_prompt': p_prompt, 'p_sample': p_sample, 'norm_pre': norm_pre, 'norm_post': norm_post,
            'w_in_a': w_in_a, 'lambda_q1': lambda_q1, 'lambda_k1': lambda_k1, 'lambda_q2': lambda_q2,
            'lambda_k2': lambda_k2, 'subln_a': subln_a, 'w_out_a': w_out_a, 'w_in_b': w_in_b,
            'conv_b': conv_b, 'a_log_b': a_log_b, 'dt_bias_b': dt_bias_b, 'onorm_b': onorm_b,
            'w_out_b': w_out_b, 'w_ple': w_ple, 'w_ple_gate': w_ple_gate}


def reference(x_prompt, x_sample, cache_k, cache_v, state_conv, state_ssm, page_table, p_prompt, p_sample,
              norm_pre, norm_post, w_in_a, lambda_q1, lambda_k1, lambda_q2, lambda_k2, subln_a, w_out_a,
              w_in_b, conv_b, a_log_b, dt_bias_b, onorm_b, w_out_b, w_ple, w_ple_gate):
    dec_b, n_pages = page_table.shape
    past_len = n_pages * cache_k.shape[2]
    dec_seq = x_sample.shape[1]
    bsz = x_prompt.shape[0]
    pos_sample = past_len + jnp.arange(dec_seq)
    pos_past = jnp.arange(past_len)
    slopes = _alibi_slopes()
    hp, hs = x_prompt, x_sample
    nk_p, nv_p, nk_s, nv_s = [], [], [], []
    nc_p, ns_p, nc_s, ns_s = [], [], [], []
    for i in range(DEPTH):
        j = i // N_MIXERS
        if i % N_MIXERS == 0:
            lam_init = 0.8 - 0.6 * math.exp(-0.3 * i)
            lam = (jnp.exp(jnp.sum(lambda_q1[j].astype(jnp.float32) * lambda_k1[j].astype(jnp.float32)))
                   - jnp.exp(jnp.sum(lambda_q2[j].astype(jnp.float32) * lambda_k2[j].astype(jnp.float32)))
                   + lam_init)
            q, k, v, gate = _diff_attn_in(_rms_norm(hp, norm_pre[i]), w_in_a[j])
            o = _diff_attn_prompt(q, k, v, lam, slopes)
            mp = _diff_attn_out(o, gate, subln_a[j], lam_init, w_out_a[j], hp.dtype)
            nk_p.append(k.reshape(k.shape[0], k.shape[1], A_HEADS, A_V_DIM))
            nv_p.append(v)
            q, k, v, gate = _diff_attn_in(_rms_norm(hs, norm_pre[i]), w_in_a[j])
            k_past = cache_k[j, page_table].reshape(dec_b, past_len, A_HEADS, 2, A_HEAD_DIM)
            v_past = cache_v[j, page_table].reshape(dec_b, past_len, A_HEADS, A_V_DIM)
            o = _diff_attn_sample(q, k, v, k_past, v_past, pos_sample, pos_past, lam, slopes)
            ms = _diff_attn_out(o, gate, subln_a[j], lam_init, w_out_a[j], hs.dtype)
            nk_s.append(k.reshape(dec_b, dec_seq, A_HEADS, A_V_DIM))
            nv_s.append(v)
        else:
            xn = _rms_norm(hp, norm_pre[i])
            buf0 = jnp.zeros((bsz, CONV_WIDTH - 1, B_CONV_CH), xn.dtype)
            s0 = jnp.zeros((bsz, B_HEADS, B_HEAD_DIM, B_HEAD_DIM), jnp.float32)
            mp, cbuf, sst = _gdn_mixer(xn, buf0, s0, w_in_b[j], conv_b[j], a_log_b[j], dt_bias_b[j],
                                       onorm_b[j], w_out_b[j])
            nc_p.append(cbuf)
            ns_p.append(sst.astype(hp.dtype))
            ms, cbuf, sst = _gdn_mixer(_rms_norm(hs, norm_pre[i]), state_conv[j], state_ssm[j], w_in_b[j],
                                       conv_b[j], a_log_b[j], dt_bias_b[j], onorm_b[j], w_out_b[j])
            nc_s.append(cbuf)
            ns_s.append(sst.astype(hs.dtype))
        hp = hp + _rms_norm(mp.astype(hp.dtype), norm_post[i])
        hs = hs + _rms_norm(ms.astype(hs.dtype), norm_post[i])
        hp = _ple(hp, p_prompt[i], w_ple[i], w_ple_gate[i])
        hs = _ple(hs, p_sample[i], w_ple[i], w_ple_gate[i])
    new_k_prompt = jnp.stack(nk_p)
    new_v_prompt = jnp.stack(nv_p)
    new_conv_prompt = jnp.stack(nc_p)
    new_ssm_prompt = jnp.stack(ns_p)
    new_k_sample = jnp.stack(nk_s)
    new_v_sample = jnp.stack(nv_s)
    new_conv_sample = jnp.stack(nc_s)
    new_ssm_sample = jnp.stack(ns_s)
    return (hp, hs, new_k_prompt, new_v_prompt, new_conv_prompt, new_ssm_prompt,
            new_k_sample, new_v_sample, new_conv_sample, new_ssm_sample)
```

```python
import functools
import math

import jax
import jax.numpy as jnp
from jax import lax
from jax.experimental import pallas as pl
from jax.experimental.pallas import tpu as pltpu

F32 = jnp.float32
BF16 = jnp.bfloat16

A_HEADS = 4
A_HEAD_DIM = 128
A_V_DIM = 2 * A_HEAD_DIM
B_HEADS = 8
B_HEAD_DIM = 128
CONV_WIDTH = 4
GDN_CHUNK = 64
NORM_EPS = 1e-6
N_MIXERS = 2

LANE = 128
SUBLANE = 8
V7X_VMEM_BYTES = 64 * 1024 * 1024
VMEM_LIMIT = 56 * 1024 * 1024

MASK_VALUE = -1e30

ROW_TILE = 256
ATTN_TILE = 512
GDN_ROWS = 256
PAGES_PER_STEP = 8
HALO = SUBLANE


def _params(semantics):
    return pltpu.CompilerParams(dimension_semantics=semantics, vmem_limit_bytes=VMEM_LIMIT)


def _full(shape):
    zeros = (0,) * len(shape)
    return pl.BlockSpec(shape, lambda *_: zeros)


def _rms(x):
    return x * lax.rsqrt(jnp.mean(x * x, axis=-1, keepdims=True) + NORM_EPS)


def _softplus(x):
    return jnp.maximum(x, 0.0) + jnp.log1p(jnp.exp(-jnp.abs(x)))


def _sigmoid(x):
    return 1.0 / (1.0 + jnp.exp(-x))


def _dot(a, b):
    return jnp.dot(a, b, preferred_element_type=F32)


def _dot_nt(a, b):
    return lax.dot_general(a, b, (((1,), (1,)), ((), ())), preferred_element_type=F32)


def _dot_tn(a, b):
    return lax.dot_general(a, b, (((0,), (0,)), ((), ())), preferred_element_type=F32)


def _dot_exact(a, b):
    return jnp.dot(a, b, preferred_element_type=F32, precision=lax.Precision.HIGHEST)


def _proj_attn_kernel(x_ref, g_ref, w_ref, q_ref, k_ref, v_ref, k16_ref, v16_ref, gate_ref, *, q_scale):
    d = x_ref.shape[1]
    xn = (_rms(x_ref[...]) * g_ref[...]).astype(BF16)
    q_ref[...] = (_dot(xn, w_ref[:, 0:d]) * q_scale).astype(q_ref.dtype)
    k = _dot(xn, w_ref[:, d:2 * d])
    k_ref[...] = k
    k16_ref[...] = k.astype(BF16)
    v = _dot(xn, w_ref[:, 2 * d:3 * d])
    v_ref[...] = v
    v16_ref[...] = v.astype(BF16)
    gate_ref[...] = _dot(xn, w_ref[:, 3 * d:4 * d])


def _proj_attn(x, gain, w16, q_dtype):
    m, d = x.shape
    tm = min(ROW_TILE, m)
    row = pl.BlockSpec((tm, d), lambda i: (i, 0))
    out_shape = (jax.ShapeDtypeStruct((m, d), q_dtype), jax.ShapeDtypeStruct((m, d), F32),
                 jax.ShapeDtypeStruct((m, d), F32), jax.ShapeDtypeStruct((m, d), BF16),
                 jax.ShapeDtypeStruct((m, d), BF16), jax.ShapeDtypeStruct((m, d), F32))
    return pl.pallas_call(
        functools.partial(_proj_attn_kernel, q_scale=A_HEAD_DIM ** -0.5),
        grid=(m // tm,),
        in_specs=[row, _full((1, d)), _full(w16.shape)],
        out_specs=(row,) * 6,
        out_shape=out_shape,
        compiler_params=_params(("parallel",)),
        name="proj_attn",
    )(x, gain, w16)


def _proj_gdn_kernel(x_ref, g_ref, w_ref, wba_ref, wbat_ref, qkv_ref, z_ref, ba_ref, bat_ref):
    d = x_ref.shape[1]
    xn = (_rms(x_ref[...]) * g_ref[...]).astype(BF16)
    for j in range(3):
        qkv_ref[:, j * d:(j + 1) * d] = _dot(xn, w_ref[:, j * d:(j + 1) * d])
    z_ref[...] = _dot(xn, w_ref[:, 3 * d:4 * d])
    ba_ref[...] = _dot(xn, wba_ref[...])
    bat_ref[...] = _dot_nt(wbat_ref[...], xn)


def _proj_gdn(x, gain, w16, wba16, wbat16):
    m, d = x.shape
    tm = min(ROW_TILE, m)
    row = pl.BlockSpec((tm, d), lambda i: (i, 0))
    nba = wbat16.shape[0]
    out_shape = (jax.ShapeDtypeStruct((m, 3 * d), F32), jax.ShapeDtypeStruct((m, d), F32),
                 jax.ShapeDtypeStruct((m, LANE), F32), jax.ShapeDtypeStruct((nba, m), F32))
    return pl.pallas_call(
        _proj_gdn_kernel,
        grid=(m // tm,),
        in_specs=[row, _full((1, d)), _full(w16.shape), _full(wba16.shape), _full(wbat16.shape)],
        out_specs=(pl.BlockSpec((tm, 3 * d), lambda i: (i, 0)), row,
                   pl.BlockSpec((tm, LANE), lambda i: (i, 0)), pl.BlockSpec((nba, tm), lambda i: (0, i))),
        out_shape=out_shape,
        compiler_params=_params(("parallel",)),
        name="proj_gdn",
    )(x, gain, w16, wba16, wbat16)


def _post_kernel(o_ref, gate_ref, h_ref, p_ref, sub_ref, npost_ref, wout_ref, wple_ref, wg_ref, out_ref,
                 *, head_dim, o_scale):
    d = o_ref.shape[1]
    o = o_ref[...]
    parts = [_rms(o[:, s:s + head_dim]) for s in range(0, d, head_dim)]
    on = jnp.concatenate(parts, axis=-1) * sub_ref[...]
    if o_scale != 1.0:
        on = on * o_scale
    gate = gate_ref[...]
    y = on * (gate * _sigmoid(gate))
    mix = _dot(y.astype(BF16), wout_ref[...])
    h1 = h_ref[...] + _rms(mix) * npost_ref[...]
    emb = _dot(p_ref[...].astype(BF16), wple_ref[...])
    egate = _sigmoid(_dot(_rms(h1).astype(BF16), wg_ref[...]))
    out_ref[...] = h1 + emb * egate


def _post(o, gate, h, p, sub_gain, post_gain, wout16, wple16, wg16, head_dim, o_scale):
    m, d = h.shape
    tm = min(ROW_TILE, m)
    row = pl.BlockSpec((tm, d), lambda i: (i, 0))
    return pl.pallas_call(
        functools.partial(_post_kernel, head_dim=head_dim, o_scale=o_scale),
        grid=(m // tm,),
        in_specs=[row, row, row, pl.BlockSpec((tm, p.shape[1]), lambda i: (i, 0)),
                  _full((1, d)), _full((1, d)), _full(wout16.shape), _full(wple16.shape), _full(wg16.shape)],
        out_specs=row,
        out_shape=jax.ShapeDtypeStruct((m, d), F32),
        compiler_params=_params(("parallel",)),
        name="post",
    )(o, gate, h, p, sub_gain, post_gain, wout16, wple16, wg16)


def _alibi_slope(h):
    return 2.0 ** (-8.0 * (h + 1) / A_HEADS)


def _lambda(lq1_ref, lk1_ref, lq2_ref, lk2_ref, lam_init):
    s1 = jnp.sum(lq1_ref[...] * lk1_ref[...], axis=-1, keepdims=True)
    s2 = jnp.sum(lq2_ref[...] * lk2_ref[...], axis=-1, keepdims=True)
    return jnp.exp(s1) - jnp.exp(s2) + lam_init


def _online_softmax_step(s, v16, m_ref, l_ref, acc_ref, idx):
    m_old = m_ref[idx]
    m_new = jnp.maximum(m_old, jnp.max(s, axis=-1, keepdims=True))
    alpha = jnp.exp(m_old - m_new)
    p = jnp.exp(s - m_new)
    l_ref[idx] = alpha * l_ref[idx] + jnp.sum(p, axis=-1, keepdims=True)
    acc_ref[idx] = alpha * acc_ref[idx] + _dot(p.astype(BF16), v16)
    m_ref[idx] = m_new


def _flash_kernel(q_ref, k_ref, v_ref, lq1_ref, lk1_ref, lq2_ref, lk2_ref, o_ref, m_ref, l_ref, acc_ref,
                  *, tile, lam_init):
    h = pl.program_id(1)
    i = pl.program_id(2)
    slope = jnp.where(h == 0, _alibi_slope(0),
                      jnp.where(h == 1, _alibi_slope(1), jnp.where(h == 2, _alibi_slope(2), _alibi_slope(3))))
    m_ref[...] = jnp.full(m_ref.shape, MASK_VALUE, F32)
    l_ref[...] = jnp.zeros(l_ref.shape, F32)
    acc_ref[...] = jnp.zeros(acc_ref.shape, F32)
    col = lax.broadcasted_iota(jnp.int32, (1, tile), 1)

    def step(j, masked):
        start = pl.multiple_of(j * tile, tile)
        k16 = k_ref[pl.ds(start, tile), :]
        v16 = v_ref[pl.ds(start, tile), :]
        bias = slope * (col + (j - i) * tile).astype(F32)
        for c in range(2):
            s = _dot_nt(q_ref[:, c * A_HEAD_DIM:(c + 1) * A_HEAD_DIM], k16[:, c * A_HEAD_DIM:(c + 1) * A_HEAD_DIM])
            s = s + bias
            if masked:
                rows = lax.broadcasted_iota(jnp.int32, (tile, tile), 0)
                cols = lax.broadcasted_iota(jnp.int32, (tile, tile), 1)
                s = jnp.where(cols <= rows, s, MASK_VALUE)
            _online_softmax_step(s, v16, m_ref, l_ref, acc_ref, c)

    def body(j, carry):
        step(j, False)
        return carry

    lax.fori_loop(0, i, body, 0)
    step(i, True)
    lam = _lambda(lq1_ref, lk1_ref, lq2_ref, lk2_ref, lam_init)
    o_ref[...] = acc_ref[0] * (1.0 / l_ref[0]) - lam * (acc_ref[1] * (1.0 / l_ref[1]))


def _flash_prompt(q16, k16, v16, lam_refs, bsz, lam_init):
    m, d = q16.shape
    seq = m // bsz
    tile = min(ATTN_TILE, seq)
    nq = seq // tile
    qspec = pl.BlockSpec((tile, A_V_DIM), lambda b, h, i: (b * nq + i, h))
    kvspec = pl.BlockSpec((seq, A_V_DIM), lambda b, h, i: (b, h))
    lspec = _full((1, A_HEAD_DIM))
    return pl.pallas_call(
        functools.partial(_flash_kernel, tile=tile, lam_init=lam_init),
        grid=(bsz, A_HEADS, nq),
        in_specs=[qspec, kvspec, kvspec, lspec, lspec, lspec, lspec],
        out_specs=qspec,
        out_shape=jax.ShapeDtypeStruct((m, d), F32),
        scratch_shapes=[pltpu.VMEM((2, tile, 1), F32), pltpu.VMEM((2, tile, 1), F32),
                        pltpu.VMEM((2, tile, A_V_DIM), F32)],
        compiler_params=_params(("parallel", "parallel", "arbitrary")),
        name="flash_prompt",
    )(q16, k16, v16, *lam_refs)


def _paged_kernel(pt_ref, q_ref, knew_ref, vnew_ref, lq1_ref, lk1_ref, lq2_ref, lk2_ref, *rest,
                  n_group, page, past_len, lam_init):
    k_refs = rest[:n_group]
    v_refs = rest[n_group:2 * n_group]
    o_ref, m_ref, l_ref, acc_ref, kpad_ref, vpad_ref = rest[2 * n_group:]
    del pt_ref
    g = pl.program_id(1)
    t = q_ref.shape[0]

    @pl.when(g == 0)
    def _():
        m_ref[...] = jnp.full(m_ref.shape, MASK_VALUE, F32)
        l_ref[...] = jnp.zeros(l_ref.shape, F32)
        acc_ref[...] = jnp.zeros(acc_ref.shape, F32)

    col = lax.broadcasted_iota(jnp.int32, (1, page), 1)

    def attend(k16, v16, kpos, masked):
        for h in range(A_HEADS):
            bias = _alibi_slope(h) * kpos
            v_h = v16[:, h * A_V_DIM:(h + 1) * A_V_DIM]
            for c in range(2):
                lo = (2 * h + c) * A_HEAD_DIM
                s = _dot_nt(q_ref[:, lo:lo + A_HEAD_DIM].astype(BF16), k16[:, lo:lo + A_HEAD_DIM]) + bias
                if masked:
                    rows = lax.broadcasted_iota(jnp.int32, (t, page), 0)
                    cols = lax.broadcasted_iota(jnp.int32, (t, page), 1)
                    s = jnp.where(cols <= rows, s, MASK_VALUE)
                _online_softmax_step(s, v_h, m_ref, l_ref, acc_ref, 2 * h + c)

    for r in range(n_group):
        kpos = (col + ((g * n_group + r) * page - past_len)).astype(F32)
        attend(k_refs[r][0].astype(BF16), v_refs[r][0].astype(BF16), kpos, False)

    @pl.when(g == pl.num_programs(1) - 1)
    def _():
        kpad_ref[...] = jnp.zeros(kpad_ref.shape, F32)
        vpad_ref[...] = jnp.zeros(vpad_ref.shape, F32)
        kpad_ref[0:t, :] = knew_ref[...]
        vpad_ref[0:t, :] = vnew_ref[...]
        attend(kpad_ref[...].astype(BF16), vpad_ref[...].astype(BF16), col.astype(F32), True)
        lam = _lambda(lq1_ref, lk1_ref, lq2_ref, lk2_ref, lam_init)
        for h in range(A_HEADS):
            o0 = acc_ref[2 * h] * (1.0 / l_ref[2 * h])
            o1 = acc_ref[2 * h + 1] * (1.0 / l_ref[2 * h + 1])
            o_ref[:, h * A_V_DIM:(h + 1) * A_V_DIM] = o0 - lam * o1


def _paged_sample(q32, k_new, v_new, cache_k, cache_v, page_rows, lam_refs, dec_b, lam_init):
    m, d = q32.shape
    t = m // dec_b
    page = cache_k.shape[1]
    n_pages = page_rows.shape[0] // dec_b
    n_group = math.gcd(PAGES_PER_STEP, n_pages)
    past_len = n_pages * page
    tok = pl.BlockSpec((t, d), lambda b, g, pt: (b, 0))
    lspec = pl.BlockSpec((1, A_HEAD_DIM), lambda b, g, pt: (0, 0))

    def page_spec(r):
        return pl.BlockSpec((1, page, d), lambda b, g, pt: (pt[b * n_pages + g * n_group + r], 0, 0))

    pages = [page_spec(r) for r in range(n_group)]
    grid_spec = pltpu.PrefetchScalarGridSpec(
        num_scalar_prefetch=1,
        grid=(dec_b, n_pages // n_group),
        in_specs=[tok, tok, tok, lspec, lspec, lspec, lspec] + pages + pages,
        out_specs=tok,
        scratch_shapes=[pltpu.VMEM((2 * A_HEADS, t, 1), F32), pltpu.VMEM((2 * A_HEADS, t, 1), F32),
                        pltpu.VMEM((2 * A_HEADS, t, A_V_DIM), F32),
                        pltpu.VMEM((page, d), F32), pltpu.VMEM((page, d), F32)],
    )
    return pl.pallas_call(
        functools.partial(_paged_kernel, n_group=n_group, page=page, past_len=past_len, lam_init=lam_init),
        grid_spec=grid_spec,
        out_shape=jax.ShapeDtypeStruct((m, d), F32),
        compiler_params=_params(("parallel", "arbitrary")),
        name="paged_sample",
    )(page_rows, q32, k_new, v_new, *lam_refs, *([cache_k] * n_group), *([cache_v] * n_group))


def _gdn_kernel(qkv_ref, ba_ref, bat_ref, convw_ref, alog_ref, dtb_ref, alogc_ref, dtbc_ref, conv0_ref, s0_ref,
                o_ref, convout_ref, sout_ref, xp_ref, y_ref, s_ref, *, chunk):
    r = pl.program_id(1)
    rows, width = qkv_ref.shape[1], qkv_ref.shape[2]
    d = width // 3
    tail = CONV_WIDTH - 1

    @pl.when(r == 0)
    def _():
        xp_ref[HALO - tail:HALO, :] = conv0_ref[0]
        s_ref[...] = s0_ref[0]

    xp_ref[HALO:HALO + rows, :] = qkv_ref[0]
    conv = convw_ref[0:1, :] * xp_ref[HALO - tail:HALO - tail + rows, :]
    for kk in range(1, CONV_WIDTH):
        conv = conv + convw_ref[kk:kk + 1, :] * xp_ref[HALO - tail + kk:HALO - tail + kk + rows, :]
    y_ref[...] = conv * _sigmoid(conv)
    last = xp_ref[HALO + rows - tail:HALO + rows, :]
    xp_ref[HALO - tail:HALO, :] = last

    @pl.when(r == pl.num_programs(1) - 1)
    def _():
        convout_ref[0] = last

    ba = ba_ref[0]
    beta_all = _sigmoid(ba)
    g_all = -jnp.exp(alog_ref[...]) * _softplus(ba + dtb_ref[...])
    g_all_t = -jnp.exp(alogc_ref[...]) * _softplus(bat_ref[0] + dtbc_ref[...])

    ii = lax.broadcasted_iota(jnp.int32, (chunk, chunk), 0)
    jj = lax.broadcasted_iota(jnp.int32, (chunk, chunk), 1)
    lower = ii >= jj
    strict = ii > jj
    tril_ones = lower.astype(F32)
    triu_ones = (ii <= jj).astype(F32)
    eye = (ii == jj).astype(F32)
    n_squarings = max(int(math.log2(chunk)) - 1, 0)

    for c0 in range(0, rows, chunk):
        gc_all = _dot_exact(tril_ones, g_all[c0:c0 + chunk, :])
        gc_all_t = _dot_exact(g_all_t[:, c0:c0 + chunk], triu_ones)
        for h in range(B_HEADS):
            q = y_ref[c0:c0 + chunk, h * B_HEAD_DIM:(h + 1) * B_HEAD_DIM]
            k = y_ref[c0:c0 + chunk, d + h * B_HEAD_DIM:d + (h + 1) * B_HEAD_DIM]
            v = y_ref[c0:c0 + chunk, 2 * d + h * B_HEAD_DIM:2 * d + (h + 1) * B_HEAD_DIM]
            q = q * lax.rsqrt(jnp.sum(q * q, axis=-1, keepdims=True) + NORM_EPS) * (B_HEAD_DIM ** -0.5)
            k = k * lax.rsqrt(jnp.sum(k * k, axis=-1, keepdims=True) + NORM_EPS)
            beta = beta_all[c0:c0 + chunk, h:h + 1]
            gc = gc_all[:, B_HEADS + h:B_HEADS + h + 1]
            gc_row = gc_all_t[B_HEADS + h:B_HEADS + h + 1, :]
            g_last = gc[chunk - 1:chunk, :]
            decay = jnp.where(lower, jnp.exp(jnp.where(lower, gc - gc_row, 0.0)), 0.0)
            egc = jnp.exp(gc)
            kb = k * beta
            vb = v * beta
            a = jnp.where(strict, _dot_nt(kb, k) * decay, 0.0)
            inv = eye - a
            power = a
            for _ in range(n_squarings):
                power = _dot(power, power)
                inv = inv + _dot(inv, power)
            uw = _dot(inv, jnp.concatenate([vb, kb * egc], axis=-1))
            u, w = uw[:, :B_HEAD_DIM], uw[:, B_HEAD_DIM:]
            qk = jnp.where(lower, _dot_nt(q, k) * decay, 0.0)
            state = s_ref[h]
            v_new = u - _dot(w, state)
            o_ref[0, c0:c0 + chunk, h * B_HEAD_DIM:(h + 1) * B_HEAD_DIM] = _dot(q * egc, state) + _dot(qk, v_new)
            k_tail = k * jnp.exp(g_last - gc)
            s_ref[h] = state * jnp.exp(g_last) + _dot_tn(k_tail, v_new)

    @pl.when(r == pl.num_programs(1) - 1)
    def _():
        sout_ref[0] = s_ref[...]


def _gdn(qkv, ba, bat, conv_w, alog_row, dtb_row, alog_col, dtb_col, conv0, s0, rows):
    bsz, seq, width = qkv.shape
    d = width // 3
    chunk = math.gcd(seq, GDN_CHUNK)
    nr = seq // rows
    nba = bat.shape[1]
    out_shape = (jax.ShapeDtypeStruct((bsz, seq, d), F32), jax.ShapeDtypeStruct(conv0.shape, F32),
                 jax.ShapeDtypeStruct(s0.shape, F32))
    return pl.pallas_call(
        functools.partial(_gdn_kernel, chunk=chunk),
        grid=(bsz, nr),
        in_specs=[pl.BlockSpec((1, rows, width), lambda b, r: (b, r, 0)),
                  pl.BlockSpec((1, rows, LANE), lambda b, r: (b, r, 0)),
                  pl.BlockSpec((1, nba, rows), lambda b, r: (b, 0, r)),
                  _full(conv_w.shape), _full(alog_row.shape), _full(dtb_row.shape),
                  _full(alog_col.shape), _full(dtb_col.shape),
                  pl.BlockSpec((1,) + conv0.shape[1:], lambda b, r: (b, 0, 0)),
                  pl.BlockSpec((1,) + s0.shape[1:], lambda b, r: (b, 0, 0, 0))],
        out_specs=(pl.BlockSpec((1, rows, d), lambda b, r: (b, r, 0)),
                   pl.BlockSpec((1,) + conv0.shape[1:], lambda b, r: (b, 0, 0)),
                   pl.BlockSpec((1,) + s0.shape[1:], lambda b, r: (b, 0, 0, 0))),
        out_shape=out_shape,
        scratch_shapes=[pltpu.VMEM((HALO + rows, width), F32), pltpu.VMEM((rows, width), F32),
                        pltpu.VMEM(s0.shape[1:], F32)],
        compiler_params=_params(("parallel", "arbitrary")),
        name="gdn",
    )(qkv, ba, bat, conv_w, alog_row, dtb_row, alog_col, dtb_col, conv0, s0)


def _lane_row(vec, offset):
    return jnp.zeros((1, LANE), F32).at[0, offset:offset + vec.shape[0]].set(vec.astype(F32))


def kernel(x_prompt, x_sample, cache_k, cache_v, state_conv, state_ssm, page_table, p_prompt, p_sample, norm_pre, norm_post, w_in_a, lambda_q1, lambda_k1, lambda_q2, lambda_k2, subln_a, w_out_a, w_in_b, conv_b, a_log_b, dt_bias_b, onorm_b, w_out_b, w_ple, w_ple_gate):
    bsz, seq, d = x_prompt.shape
    dec_b, dec_seq, _ = x_sample.shape
    depth = norm_pre.shape[0]
    n_phys, page = cache_k.shape[1], cache_k.shape[2]
    n_pages = page_table.shape[1]
    assert d == A_HEADS * A_V_DIM == B_HEADS * B_HEAD_DIM
    assert seq % GDN_ROWS == 0 and GDN_ROWS % GDN_CHUNK == 0

    hp = x_prompt.reshape(bsz * seq, d)
    hs = x_sample.reshape(dec_b * dec_seq, d)
    cache_k2 = cache_k.reshape(cache_k.shape[0] * n_phys, page, d)
    cache_v2 = cache_v.reshape(cache_v.shape[0] * n_phys, page, d)
    zero_conv = jnp.zeros((bsz, CONV_WIDTH - 1, 3 * d), F32)
    zero_ssm = jnp.zeros((bsz, B_HEADS, B_HEAD_DIM, B_HEAD_DIM), F32)

    nk_p, nv_p, nk_s, nv_s, nc_p, ns_p, nc_s, ns_s = [], [], [], [], [], [], [], []
    for i in range(depth):
        j = i // N_MIXERS
        pre_gain = norm_pre[i].reshape(1, d)
        post_gain = norm_post[i].reshape(1, d)
        wple16 = w_ple[i].astype(BF16)
        wg16 = w_ple_gate[i].astype(BF16)
        pp = p_prompt[i].reshape(bsz * seq, -1)
        ps = p_sample[i].reshape(dec_b * dec_seq, -1)
        if i % N_MIXERS == 0:
            lam_init = 0.8 - 0.6 * math.exp(-0.3 * i)
            w16 = w_in_a[j].astype(BF16)
            wout16 = w_out_a[j].astype(BF16)
            lam_refs = [t[j].reshape(1, A_HEAD_DIM) for t in (lambda_q1, lambda_k1, lambda_q2, lambda_k2)]
            sub_gain = jnp.tile(subln_a[j].reshape(1, A_V_DIM), (1, A_HEADS))
            q16, k, v, k16, v16, gate_p = _proj_attn(hp, pre_gain, w16, BF16)
            o_p = _flash_prompt(q16, k16, v16, lam_refs, bsz, lam_init)
            nk_p.append(k.reshape(bsz, seq, A_HEADS, A_V_DIM))
            nv_p.append(v.reshape(bsz, seq, A_HEADS, A_V_DIM))
            q32, k, v, _, _, gate_s = _proj_attn(hs, pre_gain, w16, F32)
            page_rows = (page_table + j * n_phys).reshape(dec_b * n_pages)
            o_s = _paged_sample(q32, k, v, cache_k2, cache_v2, page_rows, lam_refs, dec_b, lam_init)
            nk_s.append(k.reshape(dec_b, dec_seq, A_HEADS, A_V_DIM))
            nv_s.append(v.reshape(dec_b, dec_seq, A_HEADS, A_V_DIM))
            head_dim, o_scale = A_V_DIM, 1.0 - lam_init
        else:
            w16 = w_in_b[j][:, :4 * d].astype(BF16)
            wba = w_in_b[j][:, 4 * d:]
            wba16 = jnp.zeros((d, LANE), F32).at[:, :2 * B_HEADS].set(wba).astype(BF16)
            wbat16 = wba.T.astype(BF16)
            wout16 = w_out_b[j].astype(BF16)
            sub_gain = jnp.tile(onorm_b[j].reshape(1, B_HEAD_DIM), (1, B_HEADS))
            alog_row = _lane_row(a_log_b[j], B_HEADS)
            dtb_row = _lane_row(dt_bias_b[j], B_HEADS)
            alog_col = jnp.concatenate([jnp.zeros((B_HEADS,), F32), a_log_b[j]]).reshape(2 * B_HEADS, 1)
            dtb_col = jnp.concatenate([jnp.zeros((B_HEADS,), F32), dt_bias_b[j]]).reshape(2 * B_HEADS, 1)
            qkv, gate_p, ba, bat = _proj_gdn(hp, pre_gain, w16, wba16, wbat16)
            bat3 = bat.reshape(2 * B_HEADS, bsz, seq).transpose(1, 0, 2)
            o_p, cbuf, sst = _gdn(qkv.reshape(bsz, seq, 3 * d), ba.reshape(bsz, seq, LANE), bat3, conv_b[j],
                                  alog_row, dtb_row, alog_col, dtb_col, zero_conv, zero_ssm, GDN_ROWS)
            o_p = o_p.reshape(bsz * seq, d)
            nc_p.append(cbuf)
            ns_p.append(sst)
            qkv, gate_s, ba, bat = _proj_gdn(hs, pre_gain, w16, wba16, wbat16)
            bat3 = bat.reshape(2 * B_HEADS, dec_b, dec_seq).transpose(1, 0, 2)
            o_s, cbuf, sst = _gdn(qkv.reshape(dec_b, dec_seq, 3 * d), ba.reshape(dec_b, dec_seq, LANE), bat3,
                                  conv_b[j], alog_row, dtb_row, alog_col, dtb_col, state_conv[j], state_ssm[j],
                                  dec_seq)
            o_s = o_s.reshape(dec_b * dec_seq, d)
            nc_s.append(cbuf)
            ns_s.append(sst)
            head_dim, o_scale = B_HEAD_DIM, 1.0
        hp = _post(o_p, gate_p, hp, pp, sub_gain, post_gain, wout16, wple16, wg16, head_dim, o_scale)
        hs = _post(o_s, gate_s, hs, ps, sub_gain, post_gain, wout16, wple16, wg16, head_dim, o_scale)

    return (hp.reshape(bsz, seq, d), hs.reshape(dec_b, dec_seq, d),
            jnp.stack(nk_p), jnp.stack(nv_p), jnp.stack(nc_p), jnp.stack(ns_p),
            jnp.stack(nk_s), jnp.stack(nv_s), jnp.stack(nc_s), jnp.stack(ns_s))
```

```python
import functools
import math

import jax
import jax.numpy as jnp
from jax import lax
from jax.experimental import pallas as pl
from jax.experimental.pallas import tpu as pltpu

F32 = jnp.float32
BF16 = jnp.bfloat16

A_HEADS = 4
A_HEAD_DIM = 128
A_V_DIM = 2 * A_HEAD_DIM
B_HEADS = 8
B_HEAD_DIM = 128
CONV_WIDTH = 4
GDN_CHUNK = 64
NORM_EPS = 1e-6
N_MIXERS = 2

LANE = 128
SUBLANE = 8
V7X_VMEM_BYTES = 64 * 1024 * 1024
VMEM_LIMIT = V7X_VMEM_BYTES - 8 * 1024 * 1024

MASK_VALUE = -1e30
LOG2E = math.log2(math.e)

ROW_TILE = 256
ATTN_TILE = 512
GDN_ROWS = 256
PAGES_PER_STEP = 8
GDN_GROUP = 4


def _params(semantics):
    return pltpu.CompilerParams(dimension_semantics=semantics, vmem_limit_bytes=VMEM_LIMIT)


def _full(shape):
    zeros = (0,) * len(shape)
    return pl.BlockSpec(shape, lambda *_: zeros)


def _rms(x):
    return x * lax.rsqrt(jnp.mean(x * x, axis=-1, keepdims=True) + NORM_EPS)


def _softplus(x):
    return jnp.maximum(x, 0.0) + jnp.log1p(jnp.exp(-jnp.abs(x)))


def _sigmoid(x):
    return 1.0 / (1.0 + jnp.exp(-x))


def _dot(a, b):
    return jnp.dot(a, b, preferred_element_type=F32)


def _dot_nt(a, b):
    return lax.dot_general(a, b, (((1,), (1,)), ((), ())), preferred_element_type=F32)


def _dot_exact(a, b):
    return jnp.dot(a, b, preferred_element_type=F32, precision=lax.Precision.HIGHEST)


def _proj_attn_kernel(x_ref, g_ref, w_ref, q_ref, k_ref, v_ref, k16_ref, v16_ref, gate_ref, *, q_scale):
    d = x_ref.shape[1]
    xn = (_rms(x_ref[...]) * g_ref[...]).astype(BF16)
    q_ref[...] = (_dot(xn, w_ref[:, 0:d]) * q_scale).astype(q_ref.dtype)
    k = _dot(xn, w_ref[:, d:2 * d])
    k16_ref[...] = k.astype(BF16)
    v = _dot(xn, w_ref[:, 2 * d:3 * d])
    v16_ref[...] = v.astype(BF16)
    for h in range(A_HEADS):
        k_ref[:, h, :] = k[:, h * A_V_DIM:(h + 1) * A_V_DIM]
        v_ref[:, h, :] = v[:, h * A_V_DIM:(h + 1) * A_V_DIM]
    gate_ref[...] = _dot(xn, w_ref[:, 3 * d:4 * d])


def _proj_attn(x, gain, w16, q_dtype):
    m, d = x.shape
    tm = min(ROW_TILE, m)
    row = pl.BlockSpec((tm, d), lambda i: (i, 0))
    heads = pl.BlockSpec((tm, A_HEADS, A_V_DIM), lambda i: (i, 0, 0))
    out_shape = (jax.ShapeDtypeStruct((m, d), q_dtype), jax.ShapeDtypeStruct((m, A_HEADS, A_V_DIM), F32),
                 jax.ShapeDtypeStruct((m, A_HEADS, A_V_DIM), F32), jax.ShapeDtypeStruct((m, d), BF16),
                 jax.ShapeDtypeStruct((m, d), BF16), jax.ShapeDtypeStruct((m, d), F32))
    return pl.pallas_call(
        functools.partial(_proj_attn_kernel, q_scale=A_HEAD_DIM ** -0.5 * LOG2E),
        grid=(m // tm,),
        in_specs=[row, _full((1, d)), _full(w16.shape)],
        out_specs=(row, heads, heads, row, row, row),
        out_shape=out_shape,
        compiler_params=_params(("parallel",)),
        name="proj_attn",
    )(x, gain, w16)


def _proj_gdn_kernel(x_ref, g_ref, w_ref, wba_ref, qkv_ref, z_ref, ba_ref):
    d = x_ref.shape[1]
    xn = (_rms(x_ref[...]) * g_ref[...]).astype(BF16)
    for j in range(3):
        qkv_ref[:, j * d:(j + 1) * d] = _dot(xn, w_ref[:, j * d:(j + 1) * d])
    z_ref[...] = _dot(xn, w_ref[:, 3 * d:4 * d])
    ba_ref[...] = _dot(xn, wba_ref[...])


def _proj_gdn(x, gain, w16, wba16):
    m, d = x.shape
    tm = min(ROW_TILE, m)
    row = pl.BlockSpec((tm, d), lambda i: (i, 0))
    out_shape = (jax.ShapeDtypeStruct((m, 3 * d), F32), jax.ShapeDtypeStruct((m, d), F32),
                 jax.ShapeDtypeStruct((m, LANE), F32))
    return pl.pallas_call(
        _proj_gdn_kernel,
        grid=(m // tm,),
        in_specs=[row, _full((1, d)), _full(w16.shape), _full(wba16.shape)],
        out_specs=(pl.BlockSpec((tm, 3 * d), lambda i: (i, 0)), row, pl.BlockSpec((tm, LANE), lambda i: (i, 0))),
        out_shape=out_shape,
        compiler_params=_params(("parallel",)),
        name="proj_gdn",
    )(x, gain, w16, wba16)


def _post_kernel(o_ref, gate_ref, h_ref, p_ref, sub_ref, npost_ref, wout_ref, wple_ref, wg_ref, out_ref,
                 *, head_dim, o_scale):
    d = o_ref.shape[1]
    o = o_ref[...]
    parts = [_rms(o[:, s:s + head_dim]) for s in range(0, d, head_dim)]
    on = jnp.concatenate(parts, axis=-1) * sub_ref[...]
    if o_scale != 1.0:
        on = on * o_scale
    gate = gate_ref[...]
    y = on * (gate * _sigmoid(gate))
    mix = _dot(y.astype(BF16), wout_ref[...])
    h1 = h_ref[...] + _rms(mix) * npost_ref[...]
    emb = _dot(p_ref[...].astype(BF16), wple_ref[...])
    egate = _sigmoid(_dot(_rms(h1).astype(BF16), wg_ref[...]))
    out_ref[...] = h1 + emb * egate


def _post(o, gate, h, p, sub_gain, post_gain, wout16, wple16, wg16, head_dim, o_scale):
    m, d = h.shape
    tm = min(ROW_TILE, m)
    row = pl.BlockSpec((tm, d), lambda i: (i, 0))
    return pl.pallas_call(
        functools.partial(_post_kernel, head_dim=head_dim, o_scale=o_scale),
        grid=(m // tm,),
        in_specs=[row, row, row, pl.BlockSpec((tm, p.shape[1]), lambda i: (i, 0)),
                  _full((1, d)), _full((1, d)), _full(wout16.shape), _full(wple16.shape), _full(wg16.shape)],
        out_specs=row,
        out_shape=jax.ShapeDtypeStruct((m, d), F32),
        compiler_params=_params(("parallel",)),
        name="post",
    )(o, gate, h, p, sub_gain, post_gain, wout16, wple16, wg16)


def _alibi_slope(h):
    return 2.0 ** (-8.0 * (h + 1) / A_HEADS)


def _lambda(lq1_ref, lk1_ref, lq2_ref, lk2_ref, lam_init):
    s1 = jnp.sum(lq1_ref[...] * lk1_ref[...], axis=-1, keepdims=True)
    s2 = jnp.sum(lq2_ref[...] * lk2_ref[...], axis=-1, keepdims=True)
    return jnp.exp(s1) - jnp.exp(s2) + lam_init


def _online_softmax_step(s, v16, m_ref, l_ref, acc_ref, idx):
    keys = s.shape[1]
    m_prev = m_ref[idx]
    m_next = jnp.maximum(m_prev, jnp.max(s, axis=1, keepdims=True))
    p = jnp.exp2(s - (pltpu.repeat(m_next, keys // LANE, 1) if keys > LANE else m_next[:, :keys]))
    alpha = jnp.exp2(m_prev - m_next)
    l_ref[idx] = alpha * l_ref[idx] + jnp.sum(p, axis=1, keepdims=True)
    acc_ref[idx] = pltpu.repeat(alpha, acc_ref.shape[-1] // LANE, 1) * acc_ref[idx] + _dot(p.astype(BF16), v16)
    m_ref[idx] = m_next


def _normalised(acc_ref, l_ref, idx):
    return acc_ref[idx] * pltpu.repeat(1.0 / l_ref[idx], acc_ref.shape[-1] // LANE, 1)


def _flash_kernel(q_ref, k_ref, v_ref, lq1_ref, lk1_ref, lq2_ref, lk2_ref, o_ref, m_ref, l_ref, acc_ref,
                  sa_ref, sb_ref, *, tile, lam_init):
    h = pl.program_id(1)
    i = pl.program_id(2)
    slope = jnp.where(h == 0, _alibi_slope(0),
                      jnp.where(h == 1, _alibi_slope(1), jnp.where(h == 2, _alibi_slope(2), _alibi_slope(3))))
    m_ref[...] = jnp.full(m_ref.shape, MASK_VALUE, F32)
    l_ref[...] = jnp.zeros(l_ref.shape, F32)
    acc_ref[...] = jnp.zeros(acc_ref.shape, F32)
    col = lax.broadcasted_iota(jnp.int32, (1, tile), 1)

    def scores(j, s_ref):
        k16 = k_ref[pl.ds(pl.multiple_of(j * tile, tile), tile), :]
        bias = (slope * LOG2E) * (col + (j - i) * tile).astype(F32)
        for c in range(2):
            qk = _dot_nt(q_ref[:, c * A_HEAD_DIM:(c + 1) * A_HEAD_DIM], k16[:, c * A_HEAD_DIM:(c + 1) * A_HEAD_DIM])
            s_ref[c] = qk + bias

    def consume(j, s_ref, masked):
        v16 = v_ref[pl.ds(pl.multiple_of(j * tile, tile), tile), :]
        for c in range(2):
            s = s_ref[c]
            if masked:
                rows = lax.broadcasted_iota(jnp.int32, (tile, tile), 0)
                cols = lax.broadcasted_iota(jnp.int32, (tile, tile), 1)
                s = jnp.where(cols <= rows, s, MASK_VALUE)
            _online_softmax_step(s, v16, m_ref, l_ref, acc_ref, c)

    scores(0, sa_ref)

    def body(jj, carry):
        j = 2 * jj
        scores(j + 1, sb_ref)
        consume(j, sa_ref, False)
        scores(j + 2, sa_ref)
        consume(j + 1, sb_ref, False)
        return carry

    lax.fori_loop(0, i // 2, body, 0)

    @pl.when(i % 2 == 0)
    def _():
        consume(i, sa_ref, True)

    @pl.when(i % 2 == 1)
    def _():
        scores(i, sb_ref)
        consume(i - 1, sa_ref, False)
        consume(i, sb_ref, True)

    lam = _lambda(lq1_ref, lk1_ref, lq2_ref, lk2_ref, lam_init)
    o_ref[...] = _normalised(acc_ref, l_ref, 0) - lam * _normalised(acc_ref, l_ref, 1)


def _flash_prompt(q16, k16, v16, lam_refs, bsz, lam_init):
    m, d = q16.shape
    seq = m // bsz
    tile = min(ATTN_TILE, seq)
    nq = seq // tile
    qspec = pl.BlockSpec((tile, A_V_DIM), lambda b, h, i: (b * nq + i, h))
    kvspec = pl.BlockSpec((seq, A_V_DIM), lambda b, h, i: (b, h))
    lspec = _full((1, A_HEAD_DIM))
    return pl.pallas_call(
        functools.partial(_flash_kernel, tile=tile, lam_init=lam_init),
        grid=(bsz, A_HEADS, nq),
        in_specs=[qspec, kvspec, kvspec, lspec, lspec, lspec, lspec],
        out_specs=qspec,
        out_shape=jax.ShapeDtypeStruct((m, d), F32),
        scratch_shapes=[pltpu.VMEM((2, tile, LANE), F32), pltpu.VMEM((2, tile, LANE), F32),
                        pltpu.VMEM((2, tile, A_V_DIM), F32), pltpu.VMEM((2, tile, tile), F32),
                        pltpu.VMEM((2, tile, tile), F32)],
        compiler_params=_params(("parallel", "parallel", "arbitrary")),
        name="flash_prompt",
    )(q16, k16, v16, *lam_refs)


def _paged_kernel(pt_ref, q_ref, knew_ref, vnew_ref, lq1_ref, lk1_ref, lq2_ref, lk2_ref, *rest,
                  n_group, page, past_len, lam_init):
    k_refs = rest[:n_group]
    v_refs = rest[n_group:2 * n_group]
    o_ref, m_ref, l_ref, acc_ref = rest[2 * n_group:]
    del pt_ref
    g = pl.program_id(1)
    t = q_ref.shape[0]
    nrow = A_HEADS * t
    pkeys = page * A_HEADS

    @pl.when(g == 0)
    def _():
        m_ref[...] = jnp.full(m_ref.shape, MASK_VALUE, F32)
        l_ref[...] = jnp.zeros(l_ref.shape, F32)
        acc_ref[...] = jnp.zeros(acc_ref.shape, F32)

    def queries(c):
        parts = [q_ref[:, (2 * h + c) * A_HEAD_DIM:(2 * h + c + 1) * A_HEAD_DIM] for h in range(A_HEADS)]
        return jnp.concatenate(parts, axis=0).astype(BF16)

    row_head = lax.broadcasted_iota(jnp.int32, (nrow, 1), 0) // t
    slope = jnp.where(row_head == 0, _alibi_slope(0), jnp.where(row_head == 1, _alibi_slope(1),
                      jnp.where(row_head == 2, _alibi_slope(2), _alibi_slope(3)))) * LOG2E

    def attend(k16, v16, key_head, key_pos, extra_mask):
        valid = key_head == row_head
        if extra_mask is not None:
            valid = valid & extra_mask
        bias = slope * key_pos.astype(F32)
        for c in range(2):
            s = _dot_nt(queries(c), k16[:, c * A_HEAD_DIM:(c + 1) * A_HEAD_DIM])
            s = jnp.where(valid, s + bias, MASK_VALUE)
            _online_softmax_step(s, v16, m_ref, l_ref, acc_ref, c)

    keys = n_group * pkeys
    col = lax.broadcasted_iota(jnp.int32, (1, keys), 1)
    k16 = jnp.concatenate([k_refs[r][...].reshape(pkeys, A_V_DIM) for r in range(n_group)], axis=0).astype(BF16)
    v16 = jnp.concatenate([v_refs[r][...].reshape(pkeys, A_V_DIM) for r in range(n_group)], axis=0).astype(BF16)
    attend(k16, v16, col % A_HEADS, g * (n_group * page) + col // A_HEADS - past_len, None)

    @pl.when(g == pl.num_programs(1) - 1)
    def _():
        pad = jnp.zeros((LANE - nrow, A_V_DIM), F32)
        kn = jnp.concatenate([knew_ref[...].reshape(nrow, A_V_DIM), pad], axis=0)
        vn = jnp.concatenate([vnew_ref[...].reshape(nrow, A_V_DIM), pad], axis=0)
        ncol = lax.broadcasted_iota(jnp.int32, (1, LANE), 1)
        row_tok = lax.broadcasted_iota(jnp.int32, (nrow, 1), 0) % t
        causal = (ncol // A_HEADS <= row_tok) & (ncol < nrow)
        attend(kn.astype(BF16), vn.astype(BF16), ncol % A_HEADS, ncol // A_HEADS, causal)
        lam = _lambda(lq1_ref, lk1_ref, lq2_ref, lk2_ref, lam_init)
        o = _normalised(acc_ref, l_ref, 0) - lam * _normalised(acc_ref, l_ref, 1)
        for h in range(A_HEADS):
            o_ref[:, h * A_V_DIM:(h + 1) * A_V_DIM] = o[h * t:(h + 1) * t]


def _paged_sample(q32, k_new, v_new, cache_k, cache_v, page_rows, lam_refs, dec_b, lam_init):
    m, d = q32.shape
    t = m // dec_b
    page = cache_k.shape[1]
    n_pages = page_rows.shape[0] // dec_b
    n_group = math.gcd(PAGES_PER_STEP, n_pages)
    past_len = n_pages * page
    nrow = A_HEADS * t
    tok = pl.BlockSpec((t, d), lambda b, g, pt: (b, 0))
    new = pl.BlockSpec((t, A_HEADS, A_V_DIM), lambda b, g, pt: (b, 0, 0))
    lspec = pl.BlockSpec((1, A_HEAD_DIM), lambda b, g, pt: (0, 0))

    def page_spec(r):
        return pl.BlockSpec((None, page, A_HEADS, A_V_DIM),
                            lambda b, g, pt: (pt[b * n_pages + g * n_group + r], 0, 0, 0))

    pages = [page_spec(r) for r in range(n_group)]
    grid_spec = pltpu.PrefetchScalarGridSpec(
        num_scalar_prefetch=1,
        grid=(dec_b, n_pages // n_group),
        in_specs=[tok, new, new, lspec, lspec, lspec, lspec] + pages + pages,
        out_specs=tok,
        scratch_shapes=[pltpu.VMEM((2, nrow, LANE), F32), pltpu.VMEM((2, nrow, LANE), F32),
                        pltpu.VMEM((2, nrow, A_V_DIM), F32)],
    )
    return pl.pallas_call(
        functools.partial(_paged_kernel, n_group=n_group, page=page, past_len=past_len, lam_init=lam_init),
        grid_spec=grid_spec,
        out_shape=jax.ShapeDtypeStruct((m, d), F32),
        compiler_params=_params(("parallel", "arbitrary")),
        name="paged_sample",
    )(page_rows, q32, k_new, v_new, *lam_refs, *([cache_k] * n_group), *([cache_v] * n_group))


def _mm(a, b, dims=None, cast=True):
    if cast:
        a, b = a.astype(BF16), b.astype(BF16)
    if dims is None:
        return jnp.dot(a, b, preferred_element_type=F32)
    return lax.dot_general(a, b, (dims, ((), ())), preferred_element_type=F32)


_NT = ((1,), (1,))
_TN = ((0,), (0,))


def _gdn_kernel(qkv_ref, ba_ref, convw_ref, alog_ref, dtb_ref, conv0_ref, s0_ref,
                o_ref, convout_ref, sout_ref, carry_ref, y_ref, s_ref, beta_ref, gc_ref, *, chunk):
    r = pl.program_id(1)
    rows, width = qkv_ref.shape[1], qkv_ref.shape[2]
    d = width // 3
    tail = CONV_WIDTH - 1
    n_chunks = rows // chunk
    hd = B_HEAD_DIM
    cast = chunk % 16 == 0

    @pl.when(r == 0)
    def _():
        carry_ref[...] = jnp.zeros(carry_ref.shape, F32)
        carry_ref[SUBLANE - tail:SUBLANE, :] = conv0_ref[0]
        s_ref[...] = s0_ref[0]

    x = qkv_ref[0]
    x3 = x.reshape(rows // SUBLANE, SUBLANE, width)
    prev = carry_ref[...]
    sub = lax.broadcasted_iota(jnp.int32, x3.shape, 1)
    conv = convw_ref[tail:tail + 1, :] * x3
    rolled, prev_rolled = x3, prev
    for shift in range(1, CONV_WIDTH):
        rolled = pltpu.roll(rolled, 1, 1)
        prev_rolled = pltpu.roll(prev_rolled, 1, 0)
        before = prev_rolled[None]
        if rows > SUBLANE:
            before = jnp.concatenate([before, rolled[:-1]], axis=0)
        conv = conv + convw_ref[tail - shift:tail - shift + 1, :] * jnp.where(sub < shift, before, rolled)
    conv = conv.reshape(rows, width)
    half = 0.5 * conv
    y_ref[...] = half + half * jnp.tanh(half)
    carry_ref[...] = x[rows - SUBLANE:rows]

    @pl.when(r == pl.num_programs(1) - 1)
    def _():
        convout_ref[0] = qkv_ref[0, rows - tail:rows, :]

    ba = ba_ref[0]
    beta_ref[...] = _sigmoid(ba)
    g_all = -jnp.exp(alog_ref[...]) * _softplus(ba + dtb_ref[...])
    ri = lax.broadcasted_iota(jnp.int32, (rows, rows), 0)
    rj = lax.broadcasted_iota(jnp.int32, (rows, rows), 1)
    block_tril = jnp.where(((ri // chunk) == (rj // chunk)) & (ri >= rj), 1.0, 0.0)
    gc_ref[...] = _dot_exact(block_tril, g_all)

    ii = lax.broadcasted_iota(jnp.int32, (chunk, chunk), 0)
    jj = lax.broadcasted_iota(jnp.int32, (chunk, chunk), 1)
    lower = ii >= jj
    strict = ii > jj
    eye = (ii == jj).astype(F32)
    n_squarings = max(int(math.log2(chunk)) - 1, 0)
    heads = range(B_HEADS)

    def hs(h, base=0):
        return slice(base + h * hd, base + (h + 1) * hd)

    def group_step(c0, n_sub):
        pairs = [(ci, h) for ci in range(n_sub) for h in heads]
        npairs = range(len(pairs))

        def rs(ci):
            return pl.ds(c0 + ci * chunk, chunk)

        beta_c = [beta_ref[rs(ci), :] for ci in range(n_sub)]
        gc_c = [gc_ref[rs(ci), :] for ci in range(n_sub)]
        qs, ks, kbs, rhs, decays, qgs, kts, egs = [], [], [], [], [], [], [], []
        for ci, h in pairs:
            q = y_ref[rs(ci), hs(h)]
            k = y_ref[rs(ci), hs(h, d)]
            v = y_ref[rs(ci), hs(h, 2 * d)]
            q = q * (lax.rsqrt(jnp.sum(q * q, axis=-1, keepdims=True) + NORM_EPS) * (hd ** -0.5))
            k = k * lax.rsqrt(jnp.sum(k * k, axis=-1, keepdims=True) + NORM_EPS)
            beta_b = jnp.broadcast_to(beta_c[ci][:, h:h + 1], (chunk, hd))
            gc_b = jnp.broadcast_to(gc_c[ci][:, B_HEADS + h:B_HEADS + h + 1], (chunk, hd))
            gc_row = gc_b.T[0:1, 0:chunk]
            g_last_b = gc_b[chunk - 1:chunk, :]
            egc_b = jnp.exp(gc_b)
            decays.append(jnp.where(lower, jnp.exp(jnp.where(lower, gc_b[:, :chunk] - gc_row, 0.0)), 0.0))
            kb = k * beta_b
            rhs.append(jnp.concatenate([v * beta_b, kb * egc_b], axis=-1))
            qgs.append(q * egc_b)
            kts.append(k * jnp.exp(g_last_b - gc_b))
            egs.append(jnp.exp(g_last_b))
            qs.append(q)
            ks.append(k)
            kbs.append(kb)
        kk = [_mm(kbs[p], ks[p], _NT, cast) for p in npairs]
        qk = [_mm(qs[p], ks[p], _NT, cast) for p in npairs]
        qk = [jnp.where(lower, qk[p] * decays[p], 0.0) for p in npairs]
        power = [jnp.where(strict, kk[p] * decays[p], 0.0) for p in npairs]
        inv = [eye - a for a in power]
        for _ in range(n_squarings):
            power = [_mm(a, a, None, cast) for a in power]
            upd = [_mm(inv[p], power[p], None, cast) for p in npairs]
            inv = [inv[p] + upd[p] for p in npairs]
        uw = [_mm(inv[p], rhs[p], None, cast) for p in npairs]
        wq = [jnp.concatenate([uw[p][:, hd:], qgs[p]], axis=0) for p in npairs]
        for ci in range(n_sub):
            base = ci * B_HEADS
            state = [s_ref[h] for h in heads]
            ws = [_mm(wq[base + h], state[h], None, cast) for h in heads]
            v_new = [uw[base + h][:, :hd] - ws[h][:chunk] for h in heads]
            intra = [_mm(qk[base + h], v_new[h], None, cast) for h in heads]
            for h in heads:
                o_ref[0, rs(ci), hs(h)] = ws[h][chunk:] + intra[h]
            upd = [_mm(kts[base + h], v_new[h], _TN, cast) for h in heads]
            for h in heads:
                s_ref[h] = state[h] * egs[base + h] + upd[h]

    group = math.gcd(n_chunks, GDN_GROUP)
    if n_chunks == group:
        group_step(0, group)
    else:
        def body(gi, carry):
            group_step(pl.multiple_of(gi * (group * chunk), group * chunk), group)
            return carry
        lax.fori_loop(0, n_chunks // group, body, 0)

    @pl.when(r == pl.num_programs(1) - 1)
    def _():
        sout_ref[0] = s_ref[...]


def _gdn(qkv, ba, conv_w, alog_row, dtb_row, conv0, s0, rows):
    bsz, seq, width = qkv.shape
    d = width // 3
    chunk = math.gcd(seq, GDN_CHUNK)
    nr = seq // rows
    out_shape = (jax.ShapeDtypeStruct((bsz, seq, d), F32), jax.ShapeDtypeStruct(conv0.shape, F32),
                 jax.ShapeDtypeStruct(s0.shape, F32))
    return pl.pallas_call(
        functools.partial(_gdn_kernel, chunk=chunk),
        grid=(bsz, nr),
        in_specs=[pl.BlockSpec((1, rows, width), lambda b, r: (b, r, 0)),
                  pl.BlockSpec((1, rows, LANE), lambda b, r: (b, r, 0)),
                  _full(conv_w.shape), _full(alog_row.shape), _full(dtb_row.shape),
                  pl.BlockSpec((1,) + conv0.shape[1:], lambda b, r: (b, 0, 0)),
                  pl.BlockSpec((1,) + s0.shape[1:], lambda b, r: (b, 0, 0, 0))],
        out_specs=(pl.BlockSpec((1, rows, d), lambda b, r: (b, r, 0)),
                   pl.BlockSpec((1,) + conv0.shape[1:], lambda b, r: (b, 0, 0)),
                   pl.BlockSpec((1,) + s0.shape[1:], lambda b, r: (b, 0, 0, 0))),
        out_shape=out_shape,
        scratch_shapes=[pltpu.VMEM((SUBLANE, width), F32), pltpu.VMEM((rows, width), F32),
                        pltpu.VMEM(s0.shape[1:], F32), pltpu.VMEM((rows, LANE), F32),
                        pltpu.VMEM((rows, LANE), F32)],
        compiler_params=_params(("parallel", "arbitrary")),
        name="gdn",
    )(qkv, ba, conv_w, alog_row, dtb_row, conv0, s0)


def _lane_row(vec, offset):
    return jnp.zeros((1, LANE), F32).at[0, offset:offset + vec.shape[0]].set(vec.astype(F32))


def kernel(x_prompt, x_sample, cache_k, cache_v, state_conv, state_ssm, page_table, p_prompt, p_sample, norm_pre, norm_post, w_in_a, lambda_q1, lambda_k1, lambda_q2, lambda_k2, subln_a, w_out_a, w_in_b, conv_b, a_log_b, dt_bias_b, onorm_b, w_out_b, w_ple, w_ple_gate):
    bsz, seq, d = x_prompt.shape
    dec_b, dec_seq, _ = x_sample.shape
    depth = norm_pre.shape[0]
    n_phys = cache_k.shape[1]
    n_pages = page_table.shape[1]
    assert d == A_HEADS * A_V_DIM == B_HEADS * B_HEAD_DIM
    assert seq % GDN_ROWS == 0 and GDN_ROWS % GDN_CHUNK == 0

    hp = x_prompt.reshape(bsz * seq, d)
    hs = x_sample.reshape(dec_b * dec_seq, d)
    cache_k2 = cache_k.reshape((cache_k.shape[0] * n_phys,) + cache_k.shape[2:])
    cache_v2 = cache_v.reshape((cache_v.shape[0] * n_phys,) + cache_v.shape[2:])
    zero_conv = jnp.zeros((bsz, CONV_WIDTH - 1, 3 * d), F32)
    zero_ssm = jnp.zeros((bsz, B_HEADS, B_HEAD_DIM, B_HEAD_DIM), F32)

    nk_p, nv_p, nk_s, nv_s, nc_p, ns_p, nc_s, ns_s = [], [], [], [], [], [], [], []
    for i in range(depth):
        j = i // N_MIXERS
        pre_gain = norm_pre[i].reshape(1, d)
        post_gain = norm_post[i].reshape(1, d)
        wple16 = w_ple[i].astype(BF16)
        wg16 = w_ple_gate[i].astype(BF16)
        pp = p_prompt[i].reshape(bsz * seq, -1)
        ps = p_sample[i].reshape(dec_b * dec_seq, -1)
        if i % N_MIXERS == 0:
            lam_init = 0.8 - 0.6 * math.exp(-0.3 * i)
            w16 = w_in_a[j].astype(BF16)
            wout16 = w_out_a[j].astype(BF16)
            lam_refs = [t[j].reshape(1, A_HEAD_DIM) for t in (lambda_q1, lambda_k1, lambda_q2, lambda_k2)]
            sub_gain = jnp.tile(subln_a[j].reshape(1, A_V_DIM), (1, A_HEADS))
            q16, k, v, k16, v16, gate_p = _proj_attn(hp, pre_gain, w16, BF16)
            o_p = _flash_prompt(q16, k16, v16, lam_refs, bsz, lam_init)
            nk_p.append(k.reshape(bsz, seq, A_HEADS, A_V_DIM))
            nv_p.append(v.reshape(bsz, seq, A_HEADS, A_V_DIM))
            q32, k, v, _, _, gate_s = _proj_attn(hs, pre_gain, w16, F32)
            page_rows = (page_table + j * n_phys).reshape(dec_b * n_pages)
            o_s = _paged_sample(q32, k, v, cache_k2, cache_v2, page_rows, lam_refs, dec_b, lam_init)
            nk_s.append(k.reshape(dec_b, dec_seq, A_HEADS, A_V_DIM))
            nv_s.append(v.reshape(dec_b, dec_seq, A_HEADS, A_V_DIM))
            head_dim, o_scale = A_V_DIM, 1.0 - lam_init
        else:
            w16 = w_in_b[j][:, :4 * d].astype(BF16)
            wba = w_in_b[j][:, 4 * d:]
            wba16 = jnp.zeros((d, LANE), F32).at[:, :2 * B_HEADS].set(wba).astype(BF16)
            wout16 = w_out_b[j].astype(BF16)
            sub_gain = jnp.tile(onorm_b[j].reshape(1, B_HEAD_DIM), (1, B_HEADS))
            alog_row = _lane_row(a_log_b[j], B_HEADS)
            dtb_row = _lane_row(dt_bias_b[j], B_HEADS)
            qkv, gate_p, ba = _proj_gdn(hp, pre_gain, w16, wba16)
            o_p, cbuf, sst = _gdn(qkv.reshape(bsz, seq, 3 * d), ba.reshape(bsz, seq, LANE), conv_b[j],
                                  alog_row, dtb_row, zero_conv, zero_ssm, GDN_ROWS)
            o_p = o_p.reshape(bsz * seq, d)
            nc_p.append(cbuf)
            ns_p.append(sst)
            qkv, gate_s, ba = _proj_gdn(hs, pre_gain, w16, wba16)
            o_s, cbuf, sst = _gdn(qkv.reshape(dec_b, dec_seq, 3 * d), ba.reshape(dec_b, dec_seq, LANE), conv_b[j],
                                  alog_row, dtb_row, state_conv[j], state_ssm[j], dec_seq)
            o_s = o_s.reshape(dec_b * dec_seq, d)
            nc_s.append(cbuf)
            ns_s.append(sst)
            head_dim, o_scale = B_HEAD_DIM, 1.0
        hp = _post(o_p, gate_p, hp, pp, sub_gain, post_gain, wout16, wple16, wg16, head_dim, o_scale)
        hs = _post(o_s, gate_s, hs, ps, sub_gain, post_gain, wout16, wple16, wg16, head_dim, o_scale)

    return (hp.reshape(bsz, seq, d), hs.reshape(dec_b, dec_seq, d),
            jnp.stack(nk_p), jnp.stack(nv_p), jnp.stack(nc_p), jnp.stack(ns_p),
            jnp.stack(nk_s), jnp.stack(nv_s), jnp.stack(nc_s), jnp.stack(ns_s))
```

```python
import functools
import math

import jax
import jax.numpy as jnp
from jax import lax
from jax.experimental import pallas as pl
from jax.experimental.pallas import tpu as pltpu

F32 = jnp.float32
BF16 = jnp.bfloat16

A_HEADS = 4
A_HEAD_DIM = 128
A_V_DIM = 2 * A_HEAD_DIM
B_HEADS = 8
B_HEAD_DIM = 128
CONV_WIDTH = 4
GDN_CHUNK = 64
NORM_EPS = 1e-6
N_MIXERS = 2

LANE = 128
SUBLANE = 8
V7X_VMEM_BYTES = 64 * 1024 * 1024
VMEM_LIMIT = V7X_VMEM_BYTES - 8 * 1024 * 1024

MASK_VALUE = -1e30
LOG2E = math.log2(math.e)

ROW_TILE = 512
ATTN_TILE = 512
GDN_ROWS = 256
PAGES_PER_STEP = 16
GDN_GROUP = 4


def _params(semantics):
    return pltpu.CompilerParams(dimension_semantics=semantics, vmem_limit_bytes=VMEM_LIMIT)


def _full(shape):
    zeros = (0,) * len(shape)
    return pl.BlockSpec(shape, lambda *_: zeros)


def _rms(x):
    return x * lax.rsqrt(jnp.mean(x * x, axis=-1, keepdims=True) + NORM_EPS)


def _softplus(x):
    return jnp.maximum(x, 0.0) + jnp.log1p(jnp.exp(-jnp.abs(x)))


def _sigmoid(x):
    return 1.0 / (1.0 + jnp.exp(-x))


def _dot(a, b):
    return jnp.dot(a, b, preferred_element_type=F32)


def _dot_nt(a, b):
    return lax.dot_general(a, b, (((1,), (1,)), ((), ())), preferred_element_type=F32)


def _dot_exact(a, b):
    return jnp.dot(a, b, preferred_element_type=F32, precision=lax.Precision.HIGHEST)


def _proj_attn_kernel(x_ref, g_ref, w_ref, q_ref, k_ref, v_ref, k16_ref, v16_ref, gate_ref, *, q_scale):
    d = x_ref.shape[1]
    xn = (_rms(x_ref[...]) * g_ref[...]).astype(BF16)
    q_ref[...] = (_dot(xn, w_ref[:, 0:d]) * q_scale).astype(q_ref.dtype)
    k = _dot(xn, w_ref[:, d:2 * d])
    k16_ref[...] = k.astype(BF16)
    v = _dot(xn, w_ref[:, 2 * d:3 * d])
    v16_ref[...] = v.astype(BF16)
    for h in range(A_HEADS):
        k_ref[:, h, :] = k[:, h * A_V_DIM:(h + 1) * A_V_DIM]
        v_ref[:, h, :] = v[:, h * A_V_DIM:(h + 1) * A_V_DIM]
    gate_ref[...] = _dot(xn, w_ref[:, 3 * d:4 * d]).astype(gate_ref.dtype)


def _proj_attn(x, gain, w16, q_dtype):
    m, d = x.shape
    tm = min(ROW_TILE, m)
    row = pl.BlockSpec((tm, d), lambda i: (i, 0))
    heads = pl.BlockSpec((tm, A_HEADS, A_V_DIM), lambda i: (i, 0, 0))
    out_shape = (jax.ShapeDtypeStruct((m, d), q_dtype), jax.ShapeDtypeStruct((m, A_HEADS, A_V_DIM), F32),
                 jax.ShapeDtypeStruct((m, A_HEADS, A_V_DIM), F32), jax.ShapeDtypeStruct((m, d), BF16),
                 jax.ShapeDtypeStruct((m, d), BF16), jax.ShapeDtypeStruct((m, d), BF16))
    return pl.pallas_call(
        functools.partial(_proj_attn_kernel, q_scale=A_HEAD_DIM ** -0.5 * LOG2E),
        grid=(m // tm,),
        in_specs=[row, _full((1, d)), _full(w16.shape)],
        out_specs=(row, heads, heads, row, row, row),
        out_shape=out_shape,
        compiler_params=_params(("parallel",)),
        name="proj_attn",
    )(x, gain, w16)


def _proj_gdn_kernel(x_ref, g_ref, w_ref, wba_ref, qkv_ref, z_ref, ba_ref):
    d = x_ref.shape[1]
    xn = (_rms(x_ref[...]) * g_ref[...]).astype(BF16)
    for j in range(3):
        qkv_ref[:, j * d:(j + 1) * d] = _dot(xn, w_ref[:, j * d:(j + 1) * d])
    z_ref[...] = _dot(xn, w_ref[:, 3 * d:4 * d]).astype(z_ref.dtype)
    ba_ref[...] = _dot(xn, wba_ref[...])


def _proj_gdn(x, gain, w16, wba16):
    m, d = x.shape
    tm = min(ROW_TILE, m)
    row = pl.BlockSpec((tm, d), lambda i: (i, 0))
    out_shape = (jax.ShapeDtypeStruct((m, 3 * d), F32), jax.ShapeDtypeStruct((m, d), BF16),
                 jax.ShapeDtypeStruct((m, LANE), F32))
    return pl.pallas_call(
        _proj_gdn_kernel,
        grid=(m // tm,),
        in_specs=[row, _full((1, d)), _full(w16.shape), _full(wba16.shape)],
        out_specs=(pl.BlockSpec((tm, 3 * d), lambda i: (i, 0)), row, pl.BlockSpec((tm, LANE), lambda i: (i, 0))),
        out_shape=out_shape,
        compiler_params=_params(("parallel",)),
        name="proj_gdn",
    )(x, gain, w16, wba16)


def _post_kernel(o_ref, gate_ref, h_ref, p_ref, sub_ref, npost_ref, wout_ref, wple_ref, wg_ref, out_ref,
                 *, head_dim, o_scale):
    d = o_ref.shape[1]
    o = o_ref[...].astype(F32)
    parts = [_rms(o[:, s:s + head_dim]) for s in range(0, d, head_dim)]
    on = jnp.concatenate(parts, axis=-1) * sub_ref[...]
    if o_scale != 1.0:
        on = on * o_scale
    half = 0.5 * gate_ref[...].astype(F32)
    y = on * (half + half * jnp.tanh(half))
    mix = _dot(y.astype(BF16), wout_ref[...])
    h1 = h_ref[...] + _rms(mix) * npost_ref[...]
    emb = _dot(p_ref[...].astype(BF16), wple_ref[...])
    egate = 0.5 + 0.5 * jnp.tanh(0.5 * _dot(_rms(h1).astype(BF16), wg_ref[...]))
    out_ref[...] = h1 + emb * egate


def _post(o, gate, h, p, sub_gain, post_gain, wout16, wple16, wg16, head_dim, o_scale):
    m, d = h.shape
    tm = min(ROW_TILE, m)
    row = pl.BlockSpec((tm, d), lambda i: (i, 0))
    return pl.pallas_call(
        functools.partial(_post_kernel, head_dim=head_dim, o_scale=o_scale),
        grid=(m // tm,),
        in_specs=[row, row, row, pl.BlockSpec((tm, p.shape[1]), lambda i: (i, 0)),
                  _full((1, d)), _full((1, d)), _full(wout16.shape), _full(wple16.shape), _full(wg16.shape)],
        out_specs=row,
        out_shape=jax.ShapeDtypeStruct((m, d), F32),
        compiler_params=_params(("parallel",)),
        name="post",
    )(o, gate, h, p, sub_gain, post_gain, wout16, wple16, wg16)


def _alibi_slope(h):
    return 2.0 ** (-8.0 * (h + 1) / A_HEADS)


def _lambda(lq1_ref, lk1_ref, lq2_ref, lk2_ref, lam_init):
    s1 = jnp.sum(lq1_ref[...] * lk1_ref[...], axis=-1, keepdims=True)
    s2 = jnp.sum(lq2_ref[...] * lk2_ref[...], axis=-1, keepdims=True)
    return jnp.exp(s1) - jnp.exp(s2) + lam_init


def _online_softmax_step(s, v16, m_ref, l_ref, acc_ref, idx):
    keys = s.shape[1]
    m_prev = m_ref[idx]
    m_next = jnp.maximum(m_prev, jnp.max(s, axis=1, keepdims=True))
    p = jnp.exp2(s - (pltpu.repeat(m_next, keys // LANE, 1) if keys > LANE else m_next[:, :keys]))
    alpha = jnp.exp2(m_prev - m_next)
    l_ref[idx] = alpha * l_ref[idx] + jnp.sum(p, axis=1, keepdims=True)
    acc_ref[idx] = pltpu.repeat(alpha, acc_ref.shape[-1] // LANE, 1) * acc_ref[idx] + _dot(p.astype(BF16), v16)
    m_ref[idx] = m_next


def _normalised(acc_ref, l_ref, idx):
    return acc_ref[idx] * pltpu.repeat(1.0 / l_ref[idx], acc_ref.shape[-1] // LANE, 1)


def _flash_kernel(q_ref, k_ref, v_ref, lq1_ref, lk1_ref, lq2_ref, lk2_ref, o_ref, m_ref, l_ref, acc_ref,
                  sa_ref, sb_ref, *, tile, lam_init):
    h = pl.program_id(1)
    i = pl.program_id(2)
    slope = jnp.where(h == 0, _alibi_slope(0),
                      jnp.where(h == 1, _alibi_slope(1), jnp.where(h == 2, _alibi_slope(2), _alibi_slope(3))))
    m_ref[...] = jnp.full(m_ref.shape, MASK_VALUE, F32)
    l_ref[...] = jnp.zeros(l_ref.shape, F32)
    acc_ref[...] = jnp.zeros(acc_ref.shape, F32)
    col = lax.broadcasted_iota(jnp.int32, (1, tile), 1)

    def scores(j, s_ref):
        k16 = k_ref[pl.ds(pl.multiple_of(j * tile, tile), tile), :]
        bias = (slope * LOG2E) * (col + (j - i) * tile).astype(F32)
        for c in range(2):
            qk = _dot_nt(q_ref[:, c * A_HEAD_DIM:(c + 1) * A_HEAD_DIM], k16[:, c * A_HEAD_DIM:(c + 1) * A_HEAD_DIM])
            s_ref[c] = qk + bias

    def consume(j, s_ref, masked):
        v16 = v_ref[pl.ds(pl.multiple_of(j * tile, tile), tile), :]
        for c in range(2):
            s = s_ref[c]
            if masked:
                rows = lax.broadcasted_iota(jnp.int32, (tile, tile), 0)
                cols = lax.broadcasted_iota(jnp.int32, (tile, tile), 1)
                s = jnp.where(cols <= rows, s, MASK_VALUE)
            _online_softmax_step(s, v16, m_ref, l_ref, acc_ref, c)

    scores(0, sa_ref)

    def body(jj, carry):
        j = 2 * jj
        scores(j + 1, sb_ref)
        consume(j, sa_ref, False)
        scores(j + 2, sa_ref)
        consume(j + 1, sb_ref, False)
        return carry

    lax.fori_loop(0, i // 2, body, 0)

    @pl.when(i % 2 == 0)
    def _():
        consume(i, sa_ref, True)

    @pl.when(i % 2 == 1)
    def _():
        scores(i, sb_ref)
        consume(i - 1, sa_ref, False)
        consume(i, sb_ref, True)

    lam = _lambda(lq1_ref, lk1_ref, lq2_ref, lk2_ref, lam_init)
    o_ref[...] = (_normalised(acc_ref, l_ref, 0) - lam * _normalised(acc_ref, l_ref, 1)).astype(o_ref.dtype)


def _flash_prompt(q16, k16, v16, lam_refs, bsz, lam_init):
    m, d = q16.shape
    seq = m // bsz
    tile = min(ATTN_TILE, seq)
    nq = seq // tile
    qspec = pl.BlockSpec((tile, A_V_DIM), lambda b, h, i: (b * nq + i, h))
    kvspec = pl.BlockSpec((seq, A_V_DIM), lambda b, h, i: (b, h))
    lspec = _full((1, A_HEAD_DIM))
    return pl.pallas_call(
        functools.partial(_flash_kernel, tile=tile, lam_init=lam_init),
        grid=(bsz, A_HEADS, nq),
        in_specs=[qspec, kvspec, kvspec, lspec, lspec, lspec, lspec],
        out_specs=qspec,
        out_shape=jax.ShapeDtypeStruct((m, d), BF16),
        scratch_shapes=[pltpu.VMEM((2, tile, LANE), F32), pltpu.VMEM((2, tile, LANE), F32),
                        pltpu.VMEM((2, tile, A_V_DIM), F32), pltpu.VMEM((2, tile, tile), F32),
                        pltpu.VMEM((2, tile, tile), F32)],
        compiler_params=_params(("parallel", "parallel", "arbitrary")),
        name="flash_prompt",
    )(q16, k16, v16, *lam_refs)


def _paged_kernel(pt_ref, q_ref, knew_ref, vnew_ref, lq1_ref, lk1_ref, lq2_ref, lk2_ref, *rest,
                  n_group, page, past_len, lam_init):
    k_refs = rest[:n_group]
    v_refs = rest[n_group:2 * n_group]
    o_ref, m_ref, l_ref, acc_ref = rest[2 * n_group:]
    del pt_ref
    g = pl.program_id(1)
    t = q_ref.shape[0]
    nrow = A_HEADS * t
    pkeys = page * A_HEADS

    @pl.when(g == 0)
    def _():
        m_ref[...] = jnp.full(m_ref.shape, MASK_VALUE, F32)
        l_ref[...] = jnp.zeros(l_ref.shape, F32)
        acc_ref[...] = jnp.zeros(acc_ref.shape, F32)

    def queries(c):
        parts = [q_ref[:, (2 * h + c) * A_HEAD_DIM:(2 * h + c + 1) * A_HEAD_DIM] for h in range(A_HEADS)]
        return jnp.concatenate(parts, axis=0).astype(BF16)

    row_head = lax.broadcasted_iota(jnp.int32, (nrow, 1), 0) // t
    slope = jnp.where(row_head == 0, _alibi_slope(0), jnp.where(row_head == 1, _alibi_slope(1),
                      jnp.where(row_head == 2, _alibi_slope(2), _alibi_slope(3)))) * LOG2E

    def attend(k16, v16, key_head, key_pos, extra_mask):
        valid = key_head == row_head
        if extra_mask is not None:
            valid = valid & extra_mask
        bias = slope * key_pos.astype(F32)
        for c in range(2):
            s = _dot_nt(queries(c), k16[:, c * A_HEAD_DIM:(c + 1) * A_HEAD_DIM])
            s = jnp.where(valid, s + bias, MASK_VALUE)
            _online_softmax_step(s, v16, m_ref, l_ref, acc_ref, c)

    keys = n_group * pkeys
    col = lax.broadcasted_iota(jnp.int32, (1, keys), 1)
    k16 = jnp.concatenate([k_refs[r][...].reshape(pkeys, A_V_DIM) for r in range(n_group)], axis=0).astype(BF16)
    v16 = jnp.concatenate([v_refs[r][...].reshape(pkeys, A_V_DIM) for r in range(n_group)], axis=0).astype(BF16)
    attend(k16, v16, col % A_HEADS, g * (n_group * page) + col // A_HEADS - past_len, None)

    @pl.when(g == pl.num_programs(1) - 1)
    def _():
        pad = jnp.zeros((LANE - nrow, A_V_DIM), F32)
        kn = jnp.concatenate([knew_ref[...].reshape(nrow, A_V_DIM), pad], axis=0)
        vn = jnp.concatenate([vnew_ref[...].reshape(nrow, A_V_DIM), pad], axis=0)
        ncol = lax.broadcasted_iota(jnp.int32, (1, LANE), 1)
        row_tok = lax.broadcasted_iota(jnp.int32, (nrow, 1), 0) % t
        causal = (ncol // A_HEADS <= row_tok) & (ncol < nrow)
        attend(kn.astype(BF16), vn.astype(BF16), ncol % A_HEADS, ncol // A_HEADS, causal)
        lam = _lambda(lq1_ref, lk1_ref, lq2_ref, lk2_ref, lam_init)
        o = _normalised(acc_ref, l_ref, 0) - lam * _normalised(acc_ref, l_ref, 1)
        for h in range(A_HEADS):
            o_ref[:, h * A_V_DIM:(h + 1) * A_V_DIM] = o[h * t:(h + 1) * t]


def _paged_sample(q32, k_new, v_new, cache_k, cache_v, page_rows, lam_refs, dec_b, lam_init):
    m, d = q32.shape
    t = m // dec_b
    page = cache_k.shape[1]
    n_pages = page_rows.shape[0] // dec_b
    n_group = math.gcd(PAGES_PER_STEP, n_pages)
    past_len = n_pages * page
    nrow = A_HEADS * t
    tok = pl.BlockSpec((t, d), lambda b, g, pt: (b, 0))
    new = pl.BlockSpec((t, A_HEADS, A_V_DIM), lambda b, g, pt: (b, 0, 0))
    lspec = pl.BlockSpec((1, A_HEAD_DIM), lambda b, g, pt: (0, 0))

    def page_spec(r):
        return pl.BlockSpec((None, page, A_HEADS, A_V_DIM),
                            lambda b, g, pt: (pt[b * n_pages + g * n_group + r], 0, 0, 0))

    pages = [page_spec(r) for r in range(n_group)]
    grid_spec = pltpu.PrefetchScalarGridSpec(
        num_scalar_prefetch=1,
        grid=(dec_b, n_pages // n_group),
        in_specs=[tok, new, new, lspec, lspec, lspec, lspec] + pages + pages,
        out_specs=tok,
        scratch_shapes=[pltpu.VMEM((2, nrow, LANE), F32), pltpu.VMEM((2, nrow, LANE), F32),
                        pltpu.VMEM((2, nrow, A_V_DIM), F32)],
    )
    return pl.pallas_call(
        functools.partial(_paged_kernel, n_group=n_group, page=page, past_len=past_len, lam_init=lam_init),
        grid_spec=grid_spec,
        out_shape=jax.ShapeDtypeStruct((m, d), F32),
        compiler_params=_params(("parallel", "arbitrary")),
        name="paged_sample",
    )(page_rows, q32, k_new, v_new, *lam_refs, *([cache_k] * n_group), *([cache_v] * n_group))


def _mm(a, b, dims=None, cast=True):
    if cast:
        a, b = a.astype(BF16), b.astype(BF16)
    if dims is None:
        return jnp.dot(a, b, preferred_element_type=F32)
    return lax.dot_general(a, b, (dims, ((), ())), preferred_element_type=F32)


_NT = ((1,), (1,))
_TN = ((0,), (0,))


def _gdn_kernel(qkv_ref, ba_ref, convw_ref, alog_ref, dtb_ref, conv0_ref, s0_ref,
                o_ref, convout_ref, sout_ref, carry_ref, y_ref, s_ref, beta_ref, gc_ref, *, chunk):
    r = pl.program_id(1)
    rows, width = qkv_ref.shape[1], qkv_ref.shape[2]
    d = width // 3
    tail = CONV_WIDTH - 1
    n_chunks = rows // chunk
    hd = B_HEAD_DIM
    cast = chunk % 16 == 0

    @pl.when(r == 0)
    def _():
        carry_ref[...] = jnp.zeros(carry_ref.shape, F32)
        carry_ref[SUBLANE - tail:SUBLANE, :] = conv0_ref[0]
        s_ref[...] = s0_ref[0]

    x = qkv_ref[0]
    x3 = x.reshape(rows // SUBLANE, SUBLANE, width)
    prev = carry_ref[...]
    sub = lax.broadcasted_iota(jnp.int32, x3.shape, 1)
    conv = convw_ref[tail:tail + 1, :] * x3
    rolled, prev_rolled = x3, prev
    for shift in range(1, CONV_WIDTH):
        rolled = pltpu.roll(rolled, 1, 1)
        prev_rolled = pltpu.roll(prev_rolled, 1, 0)
        before = prev_rolled[None]
        if rows > SUBLANE:
            before = jnp.concatenate([before, rolled[:-1]], axis=0)
        conv = conv + convw_ref[tail - shift:tail - shift + 1, :] * jnp.where(sub < shift, before, rolled)
    conv = conv.reshape(rows, width)
    half = 0.5 * conv
    y_ref[...] = half + half * jnp.tanh(half)
    carry_ref[...] = x[rows - SUBLANE:rows]

    @pl.when(r == pl.num_programs(1) - 1)
    def _():
        convout_ref[0] = qkv_ref[0, rows - tail:rows, :]

    ba = ba_ref[0]
    beta_ref[...] = _sigmoid(ba)
    g_all = -jnp.exp(alog_ref[...]) * _softplus(ba + dtb_ref[...])
    ri = lax.broadcasted_iota(jnp.int32, (rows, rows), 0)
    rj = lax.broadcasted_iota(jnp.int32, (rows, rows), 1)
    block_tril = jnp.where(((ri // chunk) == (rj // chunk)) & (ri >= rj), 1.0, 0.0)
    gc_ref[...] = _dot_exact(block_tril, g_all)

    ii = lax.broadcasted_iota(jnp.int32, (chunk, chunk), 0)
    jj = lax.broadcasted_iota(jnp.int32, (chunk, chunk), 1)
    lower = ii >= jj
    strict = ii > jj
    eye = (ii == jj).astype(F32)
    n_squarings = max(int(math.log2(chunk)) - 1, 0)
    heads = range(B_HEADS)

    def hs(h, base=0):
        return slice(base + h * hd, base + (h + 1) * hd)

    def group_step(c0, n_sub):
        pairs = [(ci, h) for ci in range(n_sub) for h in heads]
        npairs = range(len(pairs))

        def rs(ci):
            return pl.ds(c0 + ci * chunk, chunk)

        beta_c = [beta_ref[rs(ci), :] for ci in range(n_sub)]
        gc_c = [gc_ref[rs(ci), :] for ci in range(n_sub)]
        qs, ks, kbs, rhs, decays, qgs, kts, egs = [], [], [], [], [], [], [], []
        for ci, h in pairs:
            q = y_ref[rs(ci), hs(h)]
            k = y_ref[rs(ci), hs(h, d)]
            v = y_ref[rs(ci), hs(h, 2 * d)]
            q = q * (lax.rsqrt(jnp.sum(q * q, axis=-1, keepdims=True) + NORM_EPS) * (hd ** -0.5))
            k = k * lax.rsqrt(jnp.sum(k * k, axis=-1, keepdims=True) + NORM_EPS)
            beta_b = jnp.broadcast_to(beta_c[ci][:, h:h + 1], (chunk, hd))
            gc_b = jnp.broadcast_to(gc_c[ci][:, B_HEADS + h:B_HEADS + h + 1], (chunk, hd))
            gc_row = gc_b.T[0:1, 0:chunk]
            g_last_b = gc_b[chunk - 1:chunk, :]
            egc_b = jnp.exp(gc_b)
            decays.append(jnp.where(lower, jnp.exp(jnp.where(lower, gc_b[:, :chunk] - gc_row, 0.0)), 0.0))
            kb = k * beta_b
            rhs.append(jnp.concatenate([v * beta_b, kb * egc_b], axis=-1))
            qgs.append(q * egc_b)
            kts.append(k * jnp.exp(g_last_b - gc_b))
            egs.append(jnp.exp(g_last_b))
            qs.append(q)
            ks.append(k)
            kbs.append(kb)
        kk = [_mm(kbs[p], ks[p], _NT, cast) for p in npairs]
        qk = [_mm(qs[p], ks[p], _NT, cast) for p in npairs]
        qk = [jnp.where(lower, qk[p] * decays[p], 0.0) for p in npairs]
        power = [jnp.where(strict, kk[p] * decays[p], 0.0) for p in npairs]
        inv = [eye - a for a in power]
        for _ in range(n_squarings):
            power = [_mm(a, a, None, cast) for a in power]
            upd = [_mm(inv[p], power[p], None, cast) for p in npairs]
            inv = [inv[p] + upd[p] for p in npairs]
        uw = [_mm(inv[p], rhs[p], None, cast) for p in npairs]
        wq = [jnp.concatenate([uw[p][:, hd:], qgs[p]], axis=0) for p in npairs]
        for ci in range(n_sub):
            base = ci * B_HEADS
            state = [s_ref[h] for h in heads]
            ws = [_mm(wq[base + h], state[h], None, cast) for h in heads]
            v_new = [uw[base + h][:, :hd] - ws[h][:chunk] for h in heads]
            intra = [_mm(qk[base + h], v_new[h], None, cast) for h in heads]
            for h in heads:
                o_ref[0, rs(ci), hs(h)] = (ws[h][chunk:] + intra[h]).astype(o_ref.dtype)
            upd = [_mm(kts[base + h], v_new[h], _TN, cast) for h in heads]
            for h in heads:
                s_ref[h] = state[h] * egs[base + h] + upd[h]

    group = math.gcd(n_chunks, GDN_GROUP)
    if n_chunks == group:
        group_step(0, group)
    else:
        def body(gi, carry):
            group_step(pl.multiple_of(gi * (group * chunk), group * chunk), group)
            return carry
        lax.fori_loop(0, n_chunks // group, body, 0)

    @pl.when(r == pl.num_programs(1) - 1)
    def _():
        sout_ref[0] = s_ref[...]


def _gdn(qkv, ba, conv_w, alog_row, dtb_row, conv0, s0, rows):
    bsz, seq, width = qkv.shape
    d = width // 3
    chunk = math.gcd(seq, GDN_CHUNK)
    nr = seq // rows
    o_dtype = BF16 if chunk % 16 == 0 else F32
    out_shape = (jax.ShapeDtypeStruct((bsz, seq, d), o_dtype), jax.ShapeDtypeStruct(conv0.shape, F32),
                 jax.ShapeDtypeStruct(s0.shape, F32))
    return pl.pallas_call(
        functools.partial(_gdn_kernel, chunk=chunk),
        grid=(bsz, nr),
        in_specs=[pl.BlockSpec((1, rows, width), lambda b, r: (b, r, 0)),
                  pl.BlockSpec((1, rows, LANE), lambda b, r: (b, r, 0)),
                  _full(conv_w.shape), _full(alog_row.shape), _full(dtb_row.shape),
                  pl.BlockSpec((1,) + conv0.shape[1:], lambda b, r: (b, 0, 0)),
                  pl.BlockSpec((1,) + s0.shape[1:], lambda b, r: (b, 0, 0, 0))],
        out_specs=(pl.BlockSpec((1, rows, d), lambda b, r: (b, r, 0)),
                   pl.BlockSpec((1,) + conv0.shape[1:], lambda b, r: (b, 0, 0)),
                   pl.BlockSpec((1,) + s0.shape[1:], lambda b, r: (b, 0, 0, 0))),
        out_shape=out_shape,
        scratch_shapes=[pltpu.VMEM((SUBLANE, width), F32), pltpu.VMEM((rows, width), F32),
                        pltpu.VMEM(s0.shape[1:], F32), pltpu.VMEM((rows, LANE), F32),
                        pltpu.VMEM((rows, LANE), F32)],
        compiler_params=_params(("parallel", "arbitrary")),
        name="gdn",
    )(qkv, ba, conv_w, alog_row, dtb_row, conv0, s0)


def _lane_row(vec, offset):
    return jnp.zeros((1, LANE), F32).at[0, offset:offset + vec.shape[0]].set(vec.astype(F32))


def kernel(x_prompt, x_sample, cache_k, cache_v, state_conv, state_ssm, page_table, p_prompt, p_sample, norm_pre, norm_post, w_in_a, lambda_q1, lambda_k1, lambda_q2, lambda_k2, subln_a, w_out_a, w_in_b, conv_b, a_log_b, dt_bias_b, onorm_b, w_out_b, w_ple, w_ple_gate):
    bsz, seq, d = x_prompt.shape
    dec_b, dec_seq, _ = x_sample.shape
    depth = norm_pre.shape[0]
    n_phys = cache_k.shape[1]
    n_pages = page_table.shape[1]
    assert d == A_HEADS * A_V_DIM == B_HEADS * B_HEAD_DIM
    assert seq % GDN_ROWS == 0 and GDN_ROWS % GDN_CHUNK == 0

    hp = x_prompt.reshape(bsz * seq, d)
    hs = x_sample.reshape(dec_b * dec_seq, d)
    cache_k2 = cache_k.reshape((cache_k.shape[0] * n_phys,) + cache_k.shape[2:])
    cache_v2 = cache_v.reshape((cache_v.shape[0] * n_phys,) + cache_v.shape[2:])
    zero_conv = jnp.zeros((bsz, CONV_WIDTH - 1, 3 * d), F32)
    zero_ssm = jnp.zeros((bsz, B_HEADS, B_HEAD_DIM, B_HEAD_DIM), F32)

    nk_p, nv_p, nk_s, nv_s, nc_p, ns_p, nc_s, ns_s = [], [], [], [], [], [], [], []
    for i in range(depth):
        j = i // N_MIXERS
        pre_gain = norm_pre[i].reshape(1, d)
        post_gain = norm_post[i].reshape(1, d)
        wple16 = w_ple[i].astype(BF16)
        wg16 = w_ple_gate[i].astype(BF16)
        pp = p_prompt[i].reshape(bsz * seq, -1)
        ps = p_sample[i].reshape(dec_b * dec_seq, -1)
        if i % N_MIXERS == 0:
            lam_init = 0.8 - 0.6 * math.exp(-0.3 * i)
            w16 = w_in_a[j].astype(BF16)
            wout16 = w_out_a[j].astype(BF16)
            lam_refs = [t[j].reshape(1, A_HEAD_DIM) for t in (lambda_q1, lambda_k1, lambda_q2, lambda_k2)]
            sub_gain = jnp.tile(subln_a[j].reshape(1, A_V_DIM), (1, A_HEADS))
            q16, k, v, k16, v16, gate_p = _proj_attn(hp, pre_gain, w16, BF16)
            o_p = _flash_prompt(q16, k16, v16, lam_refs, bsz, lam_init)
            nk_p.append(k.reshape(bsz, seq, A_HEADS, A_V_DIM))
            nv_p.append(v.reshape(bsz, seq, A_HEADS, A_V_DIM))
            q32, k, v, _, _, gate_s = _proj_attn(hs, pre_gain, w16, F32)
            page_rows = (page_table + j * n_phys).reshape(dec_b * n_pages)
            o_s = _paged_sample(q32, k, v, cache_k2, cache_v2, page_rows, lam_refs, dec_b, lam_init)
            nk_s.append(k.reshape(dec_b, dec_seq, A_HEADS, A_V_DIM))
            nv_s.append(v.reshape(dec_b, dec_seq, A_HEADS, A_V_DIM))
            head_dim, o_scale = A_V_DIM, 1.0 - lam_init
        else:
            w16 = w_in_b[j][:, :4 * d].astype(BF16)
            wba = w_in_b[j][:, 4 * d:]
            wba16 = jnp.zeros((d, LANE), F32).at[:, :2 * B_HEADS].set(wba).astype(BF16)
            wout16 = w_out_b[j].astype(BF16)
            sub_gain = jnp.tile(onorm_b[j].reshape(1, B_HEAD_DIM), (1, B_HEADS))
            alog_row = _lane_row(a_log_b[j], B_HEADS)
            dtb_row = _lane_row(dt_bias_b[j], B_HEADS)
            qkv, gate_p, ba = _proj_gdn(hp, pre_gain, w16, wba16)
            o_p, cbuf, sst = _gdn(qkv.reshape(bsz, seq, 3 * d), ba.reshape(bsz, seq, LANE), conv_b[j],
                                  alog_row, dtb_row, zero_conv, zero_ssm, GDN_ROWS)
            o_p = o_p.reshape(bsz * seq, d)
            nc_p.append(cbuf)
            ns_p.append(sst)
            qkv, gate_s, ba = _proj_gdn(hs, pre_gain, w16, wba16)
            o_s, cbuf, sst = _gdn(qkv.reshape(dec_b, dec_seq, 3 * d), ba.reshape(dec_b, dec_seq, LANE), conv_b[j],
                                  alog_row, dtb_row, state_conv[j], state_ssm[j], dec_seq)
            o_s = o_s.reshape(dec_b * dec_seq, d)
            nc_s.append(cbuf)
            ns_s.append(sst)
            head_dim, o_scale = B_HEAD_DIM, 1.0
        hp = _post(o_p, gate_p, hp, pp, sub_gain, post_gain, wout16, wple16, wg16, head_dim, o_scale)
        hs = _post(o_s, gate_s, hs, ps, sub_gain, post_gain, wout16, wple16, wg16, head_dim, o_scale)

    return (hp.reshape(bsz, seq, d), hs.reshape(dec_b, dec_seq, d),
            jnp.stack(nk_p), jnp.stack(nv_p), jnp.stack(nc_p), jnp.stack(ns_p),
            jnp.stack(nk_s), jnp.stack(nv_s), jnp.stack(nc_s), jnp.stack(ns_s))
```

```python
import functools
import math

import jax
import jax.numpy as jnp
from jax import lax
from jax.experimental import pallas as pl
from jax.experimental.pallas import tpu as pltpu

F32 = jnp.float32
BF16 = jnp.bfloat16

A_HEADS = 4
A_HEAD_DIM = 128
A_V_DIM = 2 * A_HEAD_DIM
B_HEADS = 8
B_HEAD_DIM = 128
CONV_WIDTH = 4
GDN_CHUNK = 64
NORM_EPS = 1e-6
N_MIXERS = 2

LANE = 128
SUBLANE = 8
V7X_VMEM_BYTES = 64 * 1024 * 1024
VMEM_LIMIT = V7X_VMEM_BYTES - 8 * 1024 * 1024

MASK_VALUE = -1e30
LOG2E = math.log2(math.e)

ROW_TILE = 512
ATTN_TILE = 512
GDN_ROWS = 256
PAGES_PER_PASS = 4
GDN_GROUP = 4


def _params(semantics):
    return pltpu.CompilerParams(dimension_semantics=semantics, vmem_limit_bytes=VMEM_LIMIT)


def _full(shape):
    zeros = (0,) * len(shape)
    return pl.BlockSpec(shape, lambda *_: zeros)


def _rms(x):
    return x * lax.rsqrt(jnp.mean(x * x, axis=-1, keepdims=True) + NORM_EPS)


def _softplus(x):
    return jnp.maximum(x, 0.0) + jnp.log1p(jnp.exp(-jnp.abs(x)))


def _sigmoid(x):
    return 1.0 / (1.0 + jnp.exp(-x))


def _dot(a, b):
    return jnp.dot(a, b, preferred_element_type=F32)


def _dot_nt(a, b):
    return lax.dot_general(a, b, (((1,), (1,)), ((), ())), preferred_element_type=F32)


def _dot_exact(a, b):
    return jnp.dot(a, b, preferred_element_type=F32, precision=lax.Precision.HIGHEST)


def _proj_attn_kernel(x_ref, g_ref, w_ref, q_ref, k_ref, v_ref, k16_ref, v16_ref, gate_ref, *, q_scale):
    d = x_ref.shape[1]
    xn = (_rms(x_ref[...]) * g_ref[...]).astype(BF16)
    q_ref[...] = (_dot(xn, w_ref[:, 0:d]) * q_scale).astype(q_ref.dtype)
    k = _dot(xn, w_ref[:, d:2 * d])
    k16_ref[...] = k.astype(BF16)
    v = _dot(xn, w_ref[:, 2 * d:3 * d])
    v16_ref[...] = v.astype(BF16)
    for h in range(A_HEADS):
        k_ref[:, h, :] = k[:, h * A_V_DIM:(h + 1) * A_V_DIM]
        v_ref[:, h, :] = v[:, h * A_V_DIM:(h + 1) * A_V_DIM]
    gate_ref[...] = _dot(xn, w_ref[:, 3 * d:4 * d]).astype(gate_ref.dtype)


def _proj_attn(x, gain, w16, q_dtype):
    m, d = x.shape
    tm = min(ROW_TILE, m)
    row = pl.BlockSpec((tm, d), lambda i: (i, 0))
    heads = pl.BlockSpec((tm, A_HEADS, A_V_DIM), lambda i: (i, 0, 0))
    out_shape = (jax.ShapeDtypeStruct((m, d), q_dtype), jax.ShapeDtypeStruct((m, A_HEADS, A_V_DIM), F32),
                 jax.ShapeDtypeStruct((m, A_HEADS, A_V_DIM), F32), jax.ShapeDtypeStruct((m, d), BF16),
                 jax.ShapeDtypeStruct((m, d), BF16), jax.ShapeDtypeStruct((m, d), BF16))
    return pl.pallas_call(
        functools.partial(_proj_attn_kernel, q_scale=A_HEAD_DIM ** -0.5 * LOG2E),
        grid=(m // tm,),
        in_specs=[row, _full((1, d)), _full(w16.shape)],
        out_specs=(row, heads, heads, row, row, row),
        out_shape=out_shape,
        compiler_params=_params(("parallel",)),
        name="proj_attn",
    )(x, gain, w16)


def _proj_gdn_kernel(x_ref, g_ref, w_ref, wba_ref, qkv_ref, z_ref, ba_ref):
    d = x_ref.shape[1]
    xn = (_rms(x_ref[...]) * g_ref[...]).astype(BF16)
    for j in range(3):
        qkv_ref[:, j * d:(j + 1) * d] = _dot(xn, w_ref[:, j * d:(j + 1) * d])
    z_ref[...] = _dot(xn, w_ref[:, 3 * d:4 * d]).astype(z_ref.dtype)
    ba_ref[...] = _dot(xn, wba_ref[...])


def _proj_gdn(x, gain, w16, wba16):
    m, d = x.shape
    tm = min(ROW_TILE, m)
    row = pl.BlockSpec((tm, d), lambda i: (i, 0))
    out_shape = (jax.ShapeDtypeStruct((m, 3 * d), F32), jax.ShapeDtypeStruct((m, d), BF16),
                 jax.ShapeDtypeStruct((m, LANE), F32))
    return pl.pallas_call(
        _proj_gdn_kernel,
        grid=(m // tm,),
        in_specs=[row, _full((1, d)), _full(w16.shape), _full(wba16.shape)],
        out_specs=(pl.BlockSpec((tm, 3 * d), lambda i: (i, 0)), row, pl.BlockSpec((tm, LANE), lambda i: (i, 0))),
        out_shape=out_shape,
        compiler_params=_params(("parallel",)),
        name="proj_gdn",
    )(x, gain, w16, wba16)


def _post_kernel(o_ref, gate_ref, h_ref, p_ref, sub_ref, npost_ref, wout_ref, wple_ref, wg_ref, out_ref,
                 *, head_dim, o_scale):
    d = o_ref.shape[1]
    o = o_ref[...].astype(F32)
    parts = [_rms(o[:, s:s + head_dim]) for s in range(0, d, head_dim)]
    on = jnp.concatenate(parts, axis=-1) * sub_ref[...]
    if o_scale != 1.0:
        on = on * o_scale
    half = 0.5 * gate_ref[...].astype(F32)
    y = on * (half + half * jnp.tanh(half))
    mix = _dot(y.astype(BF16), wout_ref[...])
    h1 = h_ref[...] + _rms(mix) * npost_ref[...]
    emb = _dot(p_ref[...].astype(BF16), wple_ref[...])
    egate = 0.5 + 0.5 * jnp.tanh(0.5 * _dot(_rms(h1).astype(BF16), wg_ref[...]))
    out_ref[...] = h1 + emb * egate


def _post(o, gate, h, p_all, layer, sub_gain, post_gain, wout16, wple16, wg16, head_dim, o_scale):
    m, d = h.shape
    tm = min(ROW_TILE, m)
    row = pl.BlockSpec((tm, d), lambda i: (i, 0))
    p_off = layer * (m // tm)
    return pl.pallas_call(
        functools.partial(_post_kernel, head_dim=head_dim, o_scale=o_scale),
        grid=(m // tm,),
        in_specs=[row, row, row, pl.BlockSpec((tm, p_all.shape[1]), lambda i: (i + p_off, 0)),
                  _full((1, d)), _full((1, d)), _full(wout16.shape), _full(wple16.shape), _full(wg16.shape)],
        out_specs=row,
        out_shape=jax.ShapeDtypeStruct((m, d), F32),
        compiler_params=_params(("parallel",)),
        name="post",
    )(o, gate, h, p_all, sub_gain, post_gain, wout16, wple16, wg16)


def _alibi_slope(h):
    return 2.0 ** (-8.0 * (h + 1) / A_HEADS)


def _lambda(lq1_ref, lk1_ref, lq2_ref, lk2_ref, lam_init):
    s1 = jnp.sum(lq1_ref[...] * lk1_ref[...], axis=-1, keepdims=True)
    s2 = jnp.sum(lq2_ref[...] * lk2_ref[...], axis=-1, keepdims=True)
    return jnp.exp(s1) - jnp.exp(s2) + lam_init


def _online_softmax_step(s, v16, m_ref, l_ref, acc_ref, idx):
    keys = s.shape[1]
    m_prev = m_ref[idx]
    m_next = jnp.maximum(m_prev, jnp.max(s, axis=1, keepdims=True))
    p = jnp.exp2(s - (pltpu.repeat(m_next, keys // LANE, 1) if keys > LANE else m_next[:, :keys]))
    alpha = jnp.exp2(m_prev - m_next)
    l_ref[idx] = alpha * l_ref[idx] + jnp.sum(p, axis=1, keepdims=True)
    acc_ref[idx] = pltpu.repeat(alpha, acc_ref.shape[-1] // LANE, 1) * acc_ref[idx] + _dot(p.astype(BF16), v16)
    m_ref[idx] = m_next


def _normalised(acc_ref, l_ref, idx):
    return acc_ref[idx] * pltpu.repeat(1.0 / l_ref[idx], acc_ref.shape[-1] // LANE, 1)


def _flash_body(h, i, q_ref, k_ref, v_ref, lam, o_ref, m_ref, l_ref, acc_ref, sa_ref, sb_ref, tile):
    slope = jnp.where(h == 0, _alibi_slope(0),
                      jnp.where(h == 1, _alibi_slope(1), jnp.where(h == 2, _alibi_slope(2), _alibi_slope(3))))
    m_ref[...] = jnp.full(m_ref.shape, MASK_VALUE, F32)
    l_ref[...] = jnp.zeros(l_ref.shape, F32)
    acc_ref[...] = jnp.zeros(acc_ref.shape, F32)
    col = lax.broadcasted_iota(jnp.int32, (1, tile), 1)

    def scores(j, s_ref):
        k16 = k_ref[pl.ds(pl.multiple_of(j * tile, tile), tile), :]
        bias = (slope * LOG2E) * (col + (j - i) * tile).astype(F32)
        for c in range(2):
            qk = _dot_nt(q_ref[:, c * A_HEAD_DIM:(c + 1) * A_HEAD_DIM], k16[:, c * A_HEAD_DIM:(c + 1) * A_HEAD_DIM])
            s_ref[c] = qk + bias

    def consume(j, s_ref, masked):
        v16 = v_ref[pl.ds(pl.multiple_of(j * tile, tile), tile), :]
        for c in range(2):
            s = s_ref[c]
            if masked:
                rows = lax.broadcasted_iota(jnp.int32, (tile, tile), 0)
                cols = lax.broadcasted_iota(jnp.int32, (tile, tile), 1)
                s = jnp.where(cols <= rows, s, MASK_VALUE)
            _online_softmax_step(s, v16, m_ref, l_ref, acc_ref, c)

    scores(0, sa_ref)

    def body(jj, carry):
        j = 2 * jj
        scores(j + 1, sb_ref)
        consume(j, sa_ref, False)
        scores(j + 2, sa_ref)
        consume(j + 1, sb_ref, False)
        return carry

    lax.fori_loop(0, i // 2, body, 0)

    @pl.when(i % 2 == 0)
    def _():
        consume(i, sa_ref, True)

    @pl.when(i % 2 == 1)
    def _():
        scores(i, sb_ref)
        consume(i - 1, sa_ref, False)
        consume(i, sb_ref, True)

    o_ref[...] = (_normalised(acc_ref, l_ref, 0) - lam * _normalised(acc_ref, l_ref, 1)).astype(o_ref.dtype)


def _paged_body(g, n_steps, q_ref, knew_ref, vnew_ref, k_refs, v_refs, lam, o_ref, m_ref, l_ref, acc_ref,
                page, past_len):
    n_group = len(k_refs)
    t = q_ref.shape[0]
    nrow = A_HEADS * t
    pkeys = page * A_HEADS

    @pl.when(g == 0)
    def _():
        m_ref[...] = jnp.full(m_ref.shape, MASK_VALUE, F32)
        l_ref[...] = jnp.zeros(l_ref.shape, F32)
        acc_ref[...] = jnp.zeros(acc_ref.shape, F32)

    def queries(c):
        parts = [q_ref[:, (2 * h + c) * A_HEAD_DIM:(2 * h + c + 1) * A_HEAD_DIM] for h in range(A_HEADS)]
        return jnp.concatenate(parts, axis=0).astype(BF16)

    row_head = lax.broadcasted_iota(jnp.int32, (nrow, 1), 0) // t
    slope = jnp.where(row_head == 0, _alibi_slope(0), jnp.where(row_head == 1, _alibi_slope(1),
                      jnp.where(row_head == 2, _alibi_slope(2), _alibi_slope(3)))) * LOG2E

    def attend(k16, v16, key_head, key_pos, extra_mask):
        valid = key_head == row_head
        if extra_mask is not None:
            valid = valid & extra_mask
        bias = slope * key_pos.astype(F32)
        for c in range(2):
            s = _dot_nt(queries(c), k16[:, c * A_HEAD_DIM:(c + 1) * A_HEAD_DIM])
            s = jnp.where(valid, s + bias, MASK_VALUE)
            _online_softmax_step(s, v16, m_ref, l_ref, acc_ref, c)

    for first in range(0, n_group, PAGES_PER_PASS):
        part = range(first, min(first + PAGES_PER_PASS, n_group))
        col = lax.broadcasted_iota(jnp.int32, (1, len(part) * pkeys), 1)
        k16 = jnp.concatenate([k_refs[r][...].reshape(pkeys, A_V_DIM) for r in part], axis=0).astype(BF16)
        v16 = jnp.concatenate([v_refs[r][...].reshape(pkeys, A_V_DIM) for r in part], axis=0).astype(BF16)
        attend(k16, v16, col % A_HEADS, (g * n_group + first) * page + col // A_HEADS - past_len, None)

    @pl.when(g == n_steps - 1)
    def _():
        pad = jnp.zeros((LANE - nrow, A_V_DIM), F32)
        kn = jnp.concatenate([knew_ref[...].reshape(nrow, A_V_DIM), pad], axis=0)
        vn = jnp.concatenate([vnew_ref[...].reshape(nrow, A_V_DIM), pad], axis=0)
        ncol = lax.broadcasted_iota(jnp.int32, (1, LANE), 1)
        row_tok = lax.broadcasted_iota(jnp.int32, (nrow, 1), 0) % t
        causal = (ncol // A_HEADS <= row_tok) & (ncol < nrow)
        attend(kn.astype(BF16), vn.astype(BF16), ncol % A_HEADS, ncol // A_HEADS, causal)
        o = _normalised(acc_ref, l_ref, 0) - lam * _normalised(acc_ref, l_ref, 1)
        for h in range(A_HEADS):
            o_ref[:, h * A_V_DIM:(h + 1) * A_V_DIM] = o[h * t:(h + 1) * t]


def _attention_kernel(pt_ref, q_ref, k_ref, v_ref, lq1_ref, lk1_ref, lq2_ref, lk2_ref, qs_ref, knew_ref, vnew_ref,
                      *rest, n_group, tile, page, past_len, lam_init):
    del pt_ref
    k_refs = rest[:n_group]
    v_refs = rest[n_group:2 * n_group]
    o_ref, os_ref, m_ref, l_ref, acc_ref, sa_ref, sb_ref, ms_ref, ls_ref, accs_ref = rest[2 * n_group:]
    b, h, i = pl.program_id(0), pl.program_id(1), pl.program_id(2)
    step = (b * pl.num_programs(1) + h) * pl.num_programs(2) + i
    n_steps = (past_len // page) // n_group
    lam = _lambda(lq1_ref, lk1_ref, lq2_ref, lk2_ref, lam_init)
    _paged_body(step % n_steps, n_steps, qs_ref, knew_ref, vnew_ref, k_refs, v_refs, lam, os_ref,
                ms_ref, ls_ref, accs_ref, page, past_len)
    _flash_body(h, i, q_ref, k_ref, v_ref, lam, o_ref, m_ref, l_ref, acc_ref, sa_ref, sb_ref, tile)


def _attention(q16, k16, v16, q32, k_new, v_new, cache_k, cache_v, page_rows, lam_refs, bsz, dec_b, lam_init):
    m, d = q16.shape
    seq = m // bsz
    tile = min(ATTN_TILE, seq)
    nq = seq // tile
    ms = q32.shape[0]
    t = ms // dec_b
    page = cache_k.shape[1]
    n_pages = page_rows.shape[0] // dec_b
    steps = bsz * A_HEADS * nq
    n_group = (dec_b * n_pages) // steps
    assert n_group * steps == dec_b * n_pages and n_group >= 1 and n_pages % n_group == 0, (steps, dec_b, n_pages)
    n_steps = n_pages // n_group
    nrow = A_HEADS * t

    def sample_seq(b, h, i):
        return ((b * A_HEADS + h) * nq + i) // n_steps

    qspec = pl.BlockSpec((tile, A_V_DIM), lambda b, h, i, pt: (b * nq + i, h))
    kvspec = pl.BlockSpec((seq, A_V_DIM), lambda b, h, i, pt: (b, h), pipeline_mode=pl.Buffered(1))
    lspec = pl.BlockSpec((1, A_HEAD_DIM), lambda b, h, i, pt: (0, 0))
    tok = pl.BlockSpec((t, d), lambda b, h, i, pt: (sample_seq(b, h, i), 0))
    new = pl.BlockSpec((t, A_HEADS, A_V_DIM), lambda b, h, i, pt: (sample_seq(b, h, i), 0, 0))

    def page_spec(r):
        return pl.BlockSpec((None, page, A_HEADS, A_V_DIM),
                            lambda b, h, i, pt: (pt[((b * A_HEADS + h) * nq + i) * n_group + r], 0, 0, 0))

    pages = [page_spec(r) for r in range(n_group)]
    grid_spec = pltpu.PrefetchScalarGridSpec(
        num_scalar_prefetch=1,
        grid=(bsz, A_HEADS, nq),
        in_specs=[qspec, kvspec, kvspec, lspec, lspec, lspec, lspec, tok, new, new] + pages + pages,
        out_specs=(qspec, tok),
        scratch_shapes=[pltpu.VMEM((2, tile, LANE), F32), pltpu.VMEM((2, tile, LANE), F32),
                        pltpu.VMEM((2, tile, A_V_DIM), F32), pltpu.VMEM((2, tile, tile), F32),
                        pltpu.VMEM((2, tile, tile), F32),
                        pltpu.VMEM((2, nrow, LANE), F32), pltpu.VMEM((2, nrow, LANE), F32),
                        pltpu.VMEM((2, nrow, A_V_DIM), F32)],
    )
    return pl.pallas_call(
        functools.partial(_attention_kernel, n_group=n_group, tile=tile, page=page, past_len=n_pages * page,
                          lam_init=lam_init),
        grid_spec=grid_spec,
        out_shape=(jax.ShapeDtypeStruct((m, d), BF16), jax.ShapeDtypeStruct((ms, d), F32)),
        compiler_params=_params(("arbitrary", "arbitrary", "arbitrary")),
        name="attention",
    )(page_rows, q16, k16, v16, *lam_refs, q32, k_new, v_new, *([cache_k] * n_group), *([cache_v] * n_group))


def _mm(a, b, dims=None, cast=True):
    if cast:
        a, b = a.astype(BF16), b.astype(BF16)
    if dims is None:
        return jnp.dot(a, b, preferred_element_type=F32)
    return lax.dot_general(a, b, (dims, ((), ())), preferred_element_type=F32)


_NT = ((1,), (1,))
_TN = ((0,), (0,))


def _gdn_kernel(qkv_ref, ba_ref, convw_ref, alog_ref, dtb_ref, conv0_ref, s0_ref,
                o_ref, convout_ref, sout_ref, carry_ref, y_ref, s_ref, beta_ref, gc_ref, *, chunk):
    r = pl.program_id(1)
    rows, width = qkv_ref.shape[1], qkv_ref.shape[2]
    d = width // 3
    tail = CONV_WIDTH - 1
    n_chunks = rows // chunk
    hd = B_HEAD_DIM
    cast = chunk % 16 == 0

    @pl.when(r == 0)
    def _():
        carry_ref[...] = jnp.zeros(carry_ref.shape, F32)
        carry_ref[SUBLANE - tail:SUBLANE, :] = conv0_ref[0]
        s_ref[...] = s0_ref[0]

    x = qkv_ref[0]
    x3 = x.reshape(rows // SUBLANE, SUBLANE, width)
    prev = carry_ref[...]
    sub = lax.broadcasted_iota(jnp.int32, x3.shape, 1)
    conv = convw_ref[tail:tail + 1, :] * x3
    rolled, prev_rolled = x3, prev
    for shift in range(1, CONV_WIDTH):
        rolled = pltpu.roll(rolled, 1, 1)
        prev_rolled = pltpu.roll(prev_rolled, 1, 0)
        before = prev_rolled[None]
        if rows > SUBLANE:
            before = jnp.concatenate([before, rolled[:-1]], axis=0)
        conv = conv + convw_ref[tail - shift:tail - shift + 1, :] * jnp.where(sub < shift, before, rolled)
    conv = conv.reshape(rows, width)
    half = 0.5 * conv
    y_ref[...] = half + half * jnp.tanh(half)
    carry_ref[...] = x[rows - SUBLANE:rows]

    @pl.when(r == pl.num_programs(1) - 1)
    def _():
        convout_ref[0] = qkv_ref[0, rows - tail:rows, :]

    ba = ba_ref[0]
    beta_ref[...] = _sigmoid(ba)
    g_all = -jnp.exp(alog_ref[...]) * _softplus(ba + dtb_ref[...])
    ri = lax.broadcasted_iota(jnp.int32, (rows, rows), 0)
    rj = lax.broadcasted_iota(jnp.int32, (rows, rows), 1)
    block_tril = jnp.where(((ri // chunk) == (rj // chunk)) & (ri >= rj), 1.0, 0.0)
    gc_ref[...] = _dot_exact(block_tril, g_all)

    ii = lax.broadcasted_iota(jnp.int32, (chunk, chunk), 0)
    jj = lax.broadcasted_iota(jnp.int32, (chunk, chunk), 1)
    lower = ii >= jj
    strict = ii > jj
    eye = (ii == jj).astype(F32)
    n_squarings = max(int(math.log2(chunk)) - 1, 0)
    heads = range(B_HEADS)

    def hs(h, base=0):
        return slice(base + h * hd, base + (h + 1) * hd)

    def group_step(c0, n_sub):
        pairs = [(ci, h) for ci in range(n_sub) for h in heads]
        npairs = range(len(pairs))

        def rs(ci):
            return pl.ds(c0 + ci * chunk, chunk)

        beta_c = [beta_ref[rs(ci), :] for ci in range(n_sub)]
        gc_c = [gc_ref[rs(ci), :] for ci in range(n_sub)]
        qs, ks, kbs, rhs, decays, qgs, kts, egs = [], [], [], [], [], [], [], []
        for ci, h in pairs:
            q = y_ref[rs(ci), hs(h)]
            k = y_ref[rs(ci), hs(h, d)]
            v = y_ref[rs(ci), hs(h, 2 * d)]
            q = q * (lax.rsqrt(jnp.sum(q * q, axis=-1, keepdims=True) + NORM_EPS) * (hd ** -0.5))
            k = k * lax.rsqrt(jnp.sum(k * k, axis=-1, keepdims=True) + NORM_EPS)
            beta_b = jnp.broadcast_to(beta_c[ci][:, h:h + 1], (chunk, hd))
            gc_b = jnp.broadcast_to(gc_c[ci][:, B_HEADS + h:B_HEADS + h + 1], (chunk, hd))
            gc_row = gc_b.T[0:1, 0:chunk]
            g_last_b = gc_b[chunk - 1:chunk, :]
            egc_b = jnp.exp(gc_b)
            decays.append(jnp.where(lower, jnp.exp(jnp.where(lower, gc_b[:, :chunk] - gc_row, 0.0)), 0.0))
            kb = k * beta_b
            rhs.append(jnp.concatenate([v * beta_b, kb * egc_b], axis=-1))
            qgs.append(q * egc_b)
            kts.append(k * jnp.exp(g_last_b - gc_b))
            egs.append(jnp.exp(g_last_b))
            qs.append(q)
            ks.append(k)
            kbs.append(kb)
        kk = [_mm(kbs[p], ks[p], _NT, cast) for p in npairs]
        qk = [_mm(qs[p], ks[p], _NT, cast) for p in npairs]
        qk = [jnp.where(lower, qk[p] * decays[p], 0.0) for p in npairs]
        power = [jnp.where(strict, kk[p] * decays[p], 0.0) for p in npairs]
        inv = [eye - a for a in power]
        for _ in range(n_squarings):
            power = [_mm(a, a, None, cast) for a in power]
            upd = [_mm(inv[p], power[p], None, cast) for p in npairs]
            inv = [inv[p] + upd[p] for p in npairs]
        uw = [_mm(inv[p], rhs[p], None, cast) for p in npairs]
        wq = [jnp.concatenate([uw[p][:, hd:], qgs[p]], axis=0) for p in npairs]
        for ci in range(n_sub):
            base = ci * B_HEADS
            state = [s_ref[h] for h in heads]
            ws = [_mm(wq[base + h], state[h], None, cast) for h in heads]
            v_new = [uw[base + h][:, :hd] - ws[h][:chunk] for h in heads]
            intra = [_mm(qk[base + h], v_new[h], None, cast) for h in heads]
            for h in heads:
                o_ref[0, rs(ci), hs(h)] = (ws[h][chunk:] + intra[h]).astype(o_ref.dtype)
            upd = [_mm(kts[base + h], v_new[h], _TN, cast) for h in heads]
            for h in heads:
                s_ref[h] = state[h] * egs[base + h] + upd[h]

    group = math.gcd(n_chunks, GDN_GROUP)
    if n_chunks == group:
        group_step(0, group)
    else:
        def body(gi, carry):
            group_step(pl.multiple_of(gi * (group * chunk), group * chunk), group)
            return carry
        lax.fori_loop(0, n_chunks // group, body, 0)

    @pl.when(r == pl.num_programs(1) - 1)
    def _():
        sout_ref[0] = s_ref[...]


def _gdn(qkv, ba, conv_w, alog_row, dtb_row, conv0, s0, s0_off, rows):
    bsz, seq, width = qkv.shape
    d = width // 3
    chunk = math.gcd(seq, GDN_CHUNK)
    nr = seq // rows
    o_dtype = BF16 if chunk % 16 == 0 else F32
    out_shape = (jax.ShapeDtypeStruct((bsz, seq, d), o_dtype), jax.ShapeDtypeStruct(conv0.shape, F32),
                 jax.ShapeDtypeStruct((bsz,) + s0.shape[1:], F32))
    return pl.pallas_call(
        functools.partial(_gdn_kernel, chunk=chunk),
        grid=(bsz, nr),
        in_specs=[pl.BlockSpec((1, rows, width), lambda b, r: (b, r, 0)),
                  pl.BlockSpec((1, rows, LANE), lambda b, r: (b, r, 0)),
                  _full(conv_w.shape), _full(alog_row.shape), _full(dtb_row.shape),
                  pl.BlockSpec((1,) + conv0.shape[1:], lambda b, r: (b, 0, 0)),
                  pl.BlockSpec((1,) + s0.shape[1:], lambda b, r: (b + s0_off, 0, 0, 0))],
        out_specs=(pl.BlockSpec((1, rows, d), lambda b, r: (b, r, 0)),
                   pl.BlockSpec((1,) + conv0.shape[1:], lambda b, r: (b, 0, 0)),
                   pl.BlockSpec((1,) + s0.shape[1:], lambda b, r: (b, 0, 0, 0))),
        out_shape=out_shape,
        scratch_shapes=[pltpu.VMEM((SUBLANE, width), F32), pltpu.VMEM((rows, width), F32),
                        pltpu.VMEM(s0.shape[1:], F32), pltpu.VMEM((rows, LANE), F32),
                        pltpu.VMEM((rows, LANE), F32)],
        compiler_params=_params(("parallel", "arbitrary")),
        name="gdn",
    )(qkv, ba, conv_w, alog_row, dtb_row, conv0, s0)


def _lane_row(vec, offset):
    return jnp.zeros((1, LANE), F32).at[0, offset:offset + vec.shape[0]].set(vec.astype(F32))


def kernel(x_prompt, x_sample, cache_k, cache_v, state_conv, state_ssm, page_table, p_prompt, p_sample, norm_pre, norm_post, w_in_a, lambda_q1, lambda_k1, lambda_q2, lambda_k2, subln_a, w_out_a, w_in_b, conv_b, a_log_b, dt_bias_b, onorm_b, w_out_b, w_ple, w_ple_gate):
    bsz, seq, d = x_prompt.shape
    dec_b, dec_seq, _ = x_sample.shape
    depth = norm_pre.shape[0]
    n_phys = cache_k.shape[1]
    n_pages = page_table.shape[1]
    assert d == A_HEADS * A_V_DIM == B_HEADS * B_HEAD_DIM
    assert seq % GDN_ROWS == 0 and GDN_ROWS % GDN_CHUNK == 0

    hp = x_prompt.reshape(bsz * seq, d)
    hs = x_sample.reshape(dec_b * dec_seq, d)
    cache_k2 = cache_k.reshape((cache_k.shape[0] * n_phys,) + cache_k.shape[2:])
    cache_v2 = cache_v.reshape((cache_v.shape[0] * n_phys,) + cache_v.shape[2:])
    zero_conv = jnp.zeros((bsz, CONV_WIDTH - 1, 3 * d), F32)
    zero_ssm = jnp.zeros((bsz, B_HEADS, B_HEAD_DIM, B_HEAD_DIM), F32)
    ssm_all = state_ssm.reshape((state_ssm.shape[0] * dec_b,) + state_ssm.shape[2:])
    pp_all = p_prompt.reshape(depth * bsz * seq, -1)
    ps_all = p_sample.reshape(depth * dec_b * dec_seq, -1)

    nk_p, nv_p, nk_s, nv_s, nc_p, ns_p, nc_s, ns_s = [], [], [], [], [], [], [], []
    for i in range(depth):
        j = i // N_MIXERS
        pre_gain = norm_pre[i].reshape(1, d)
        post_gain = norm_post[i].reshape(1, d)
        wple16 = w_ple[i].astype(BF16)
        wg16 = w_ple_gate[i].astype(BF16)
        if i % N_MIXERS == 0:
            lam_init = 0.8 - 0.6 * math.exp(-0.3 * i)
            w16 = w_in_a[j].astype(BF16)
            wout16 = w_out_a[j].astype(BF16)
            lam_refs = [t[j].reshape(1, A_HEAD_DIM) for t in (lambda_q1, lambda_k1, lambda_q2, lambda_k2)]
            sub_gain = jnp.tile(subln_a[j].reshape(1, A_V_DIM), (1, A_HEADS))
            q16, k, v, k16, v16, gate_p = _proj_attn(hp, pre_gain, w16, BF16)
            nk_p.append(k.reshape(bsz, seq, A_HEADS, A_V_DIM))
            nv_p.append(v.reshape(bsz, seq, A_HEADS, A_V_DIM))
            q32, k, v, _, _, gate_s = _proj_attn(hs, pre_gain, w16, F32)
            nk_s.append(k.reshape(dec_b, dec_seq, A_HEADS, A_V_DIM))
            nv_s.append(v.reshape(dec_b, dec_seq, A_HEADS, A_V_DIM))
            page_rows = (page_table + j * n_phys).reshape(dec_b * n_pages)
            o_p, o_s = _attention(q16, k16, v16, q32, k, v, cache_k2, cache_v2, page_rows, lam_refs, bsz, dec_b,
                                  lam_init)
            head_dim, o_scale = A_V_DIM, 1.0 - lam_init
        else:
            w16 = w_in_b[j][:, :4 * d].astype(BF16)
            wba = w_in_b[j][:, 4 * d:]
            wba16 = jnp.zeros((d, LANE), F32).at[:, :2 * B_HEADS].set(wba).astype(BF16)
            wout16 = w_out_b[j].astype(BF16)
            sub_gain = jnp.tile(onorm_b[j].reshape(1, B_HEAD_DIM), (1, B_HEADS))
            alog_row = _lane_row(a_log_b[j], B_HEADS)
            dtb_row = _lane_row(dt_bias_b[j], B_HEADS)
            qkv, gate_p, ba = _proj_gdn(hp, pre_gain, w16, wba16)
            o_p, cbuf, sst = _gdn(qkv.reshape(bsz, seq, 3 * d), ba.reshape(bsz, seq, LANE), conv_b[j],
                                  alog_row, dtb_row, zero_conv, zero_ssm, 0, GDN_ROWS)
            o_p = o_p.reshape(bsz * seq, d)
            nc_p.append(cbuf)
            ns_p.append(sst)
            qkv, gate_s, ba = _proj_gdn(hs, pre_gain, w16, wba16)
            o_s, cbuf, sst = _gdn(qkv.reshape(dec_b, dec_seq, 3 * d), ba.reshape(dec_b, dec_seq, LANE), conv_b[j],
                                  alog_row, dtb_row, state_conv[j], ssm_all, j * dec_b, dec_seq)
            o_s = o_s.reshape(dec_b * dec_seq, d)
            nc_s.append(cbuf)
            ns_s.append(sst)
            head_dim, o_scale = B_HEAD_DIM, 1.0
        hp = _post(o_p, gate_p, hp, pp_all, i, sub_gain, post_gain, wout16, wple16, wg16, head_dim, o_scale)
        hs = _post(o_s, gate_s, hs, ps_all, i, sub_gain, post_gain, wout16, wple16, wg16, head_dim, o_scale)

    return (hp.reshape(bsz, seq, d), hs.reshape(dec_b, dec_seq, d),
            jnp.stack(nk_p), jnp.stack(nv_p), jnp.stack(nc_p), jnp.stack(ns_p),
            jnp.stack(nk_s), jnp.stack(nv_s), jnp.stack(nc_s), jnp.stack(ns_s))
```

```python
import functools
import math

import jax
import jax.numpy as jnp
from jax import lax
from jax.experimental import pallas as pl
from jax.experimental.pallas import tpu as pltpu

F32 = jnp.float32
BF16 = jnp.bfloat16

A_HEADS = 4
A_HEAD_DIM = 128
A_V_DIM = 2 * A_HEAD_DIM
B_HEADS = 8
B_HEAD_DIM = 128
CONV_WIDTH = 4
GDN_CHUNK = 64
NORM_EPS = 1e-6
N_MIXERS = 2

LANE = 128
SUBLANE = 8
V7X_VMEM_BYTES = 64 * 1024 * 1024
VMEM_LIMIT = V7X_VMEM_BYTES - 8 * 1024 * 1024

MASK_VALUE = -1e30
LOG2E = math.log2(math.e)

ROW_TILE = 512
ATTN_TILE = 512
GDN_ROWS = 256
PAGES_PER_PASS = 4
GDN_GROUP = 4


def _params(semantics):
    return pltpu.CompilerParams(dimension_semantics=semantics, vmem_limit_bytes=VMEM_LIMIT)


def _full(shape):
    zeros = (0,) * len(shape)
    return pl.BlockSpec(shape, lambda *_: zeros)


def _rms(x):
    return x * lax.rsqrt(jnp.mean(x * x, axis=-1, keepdims=True) + NORM_EPS)


def _softplus(x):
    return jnp.maximum(x, 0.0) + jnp.log1p(jnp.exp(-jnp.abs(x)))


def _sigmoid(x):
    return 1.0 / (1.0 + jnp.exp(-x))


def _dot(a, b):
    return jnp.dot(a, b, preferred_element_type=F32)


def _dot_nt(a, b):
    return lax.dot_general(a, b, (((1,), (1,)), ((), ())), preferred_element_type=F32)


def _dot_exact(a, b):
    return jnp.dot(a, b, preferred_element_type=F32, precision=lax.Precision.HIGHEST)


def _proj_attn_kernel(x_ref, g_ref, w_ref, *rest, q_scale, slot, n_slots, first):
    if first:
        q_ref, k_ref, v_ref, k16_ref, v16_ref, gate_ref = rest
    else:
        q_ref, k_ref, v_ref, k16_ref, v16_ref, gate_ref = rest[2:]
    d = x_ref.shape[1]
    xn = (_rms(x_ref[...]) * g_ref[...]).astype(BF16)
    q_ref[...] = (_dot(xn, w_ref[:, 0:d]) * q_scale).astype(q_ref.dtype)
    k = _dot(xn, w_ref[:, d:2 * d])
    k16_ref[...] = k.astype(BF16)
    v = _dot(xn, w_ref[:, 2 * d:3 * d])
    v16_ref[...] = v.astype(BF16)
    for h in range(A_HEADS):
        if first:
            k_ref[slot, :, h, :] = k[:, h * A_V_DIM:(h + 1) * A_V_DIM]
            v_ref[slot, :, h, :] = v[:, h * A_V_DIM:(h + 1) * A_V_DIM]
        else:
            k_ref[:, h, :] = k[:, h * A_V_DIM:(h + 1) * A_V_DIM]
            v_ref[:, h, :] = v[:, h * A_V_DIM:(h + 1) * A_V_DIM]
    if first:
        for other in range(n_slots):
            if other != slot:
                k_ref[other] = jnp.zeros(k_ref.shape[1:], F32)
                v_ref[other] = jnp.zeros(v_ref.shape[1:], F32)
    gate_ref[...] = _dot(xn, w_ref[:, 3 * d:4 * d]).astype(gate_ref.dtype)


def _proj_attn(x, gains, layer, w16_all, slot, n_slots, kv_prev, q_dtype):
    m, d = x.shape
    tm = min(ROW_TILE, m)
    first = kv_prev is None
    row = pl.BlockSpec((tm, d), lambda i: (i, 0))
    gain_spec = pl.BlockSpec((None, 1, d), lambda i: (layer, 0, 0))
    w_spec = pl.BlockSpec((None,) + w16_all.shape[1:], lambda i: (slot, 0, 0))
    if first:
        heads = pl.BlockSpec((n_slots, tm, A_HEADS, A_V_DIM), lambda i: (0, i, 0, 0))
        extra_in, extra_specs, aliases = (), [], {}
    else:
        heads = pl.BlockSpec((None, tm, A_HEADS, A_V_DIM), lambda i: (slot, i, 0, 0))
        extra_in, extra_specs = tuple(kv_prev), [pl.BlockSpec(memory_space=pl.ANY)] * 2
        aliases = {3: 1, 4: 2}
    stacked = jax.ShapeDtypeStruct((n_slots, m, A_HEADS, A_V_DIM), F32)
    out_shape = (jax.ShapeDtypeStruct((m, d), q_dtype), stacked, stacked, jax.ShapeDtypeStruct((m, d), BF16),
                 jax.ShapeDtypeStruct((m, d), BF16), jax.ShapeDtypeStruct((m, d), BF16))
    return pl.pallas_call(
        functools.partial(_proj_attn_kernel, q_scale=A_HEAD_DIM ** -0.5 * LOG2E, slot=slot, n_slots=n_slots,
                          first=first),
        grid=(m // tm,),
        in_specs=[row, gain_spec, w_spec] + extra_specs,
        out_specs=(row, heads, heads, row, row, row),
        out_shape=out_shape,
        input_output_aliases=aliases,
        compiler_params=_params(("parallel",)),
        name="proj_attn",
    )(x, gains, w16_all, *extra_in)


def _proj_gdn_kernel(x_ref, g_ref, w_ref, wba_ref, qkv_ref, z_ref, ba_ref):
    d = x_ref.shape[1]
    xn = (_rms(x_ref[...]) * g_ref[...]).astype(BF16)
    for j in range(3):
        qkv_ref[:, j * d:(j + 1) * d] = _dot(xn, w_ref[:, j * d:(j + 1) * d])
    z_ref[...] = _dot(xn, w_ref[:, 3 * d:4 * d]).astype(z_ref.dtype)
    ba_ref[...] = _dot(xn, wba_ref[...])


def _proj_gdn(x, gains, layer, w16, wba16):
    m, d = x.shape
    tm = min(ROW_TILE, m)
    row = pl.BlockSpec((tm, d), lambda i: (i, 0))
    out_shape = (jax.ShapeDtypeStruct((m, 3 * d), F32), jax.ShapeDtypeStruct((m, d), BF16),
                 jax.ShapeDtypeStruct((m, LANE), F32))
    return pl.pallas_call(
        _proj_gdn_kernel,
        grid=(m // tm,),
        in_specs=[row, pl.BlockSpec((None, 1, d), lambda i: (layer, 0, 0)), _full(w16.shape), _full(wba16.shape)],
        out_specs=(pl.BlockSpec((tm, 3 * d), lambda i: (i, 0)), row, pl.BlockSpec((tm, LANE), lambda i: (i, 0))),
        out_shape=out_shape,
        compiler_params=_params(("parallel",)),
        name="proj_gdn",
    )(x, gains, w16, wba16)


def _post_kernel(o_ref, gate_ref, h_ref, p_ref, sub_ref, npost_ref, wout_ref, wple_ref, wg_ref, out_ref,
                 *, head_dim, o_scale):
    d = o_ref.shape[1]
    o = o_ref[...].astype(F32)
    parts = [_rms(o[:, s:s + head_dim]) for s in range(0, d, head_dim)]
    on = jnp.concatenate(parts, axis=-1) * sub_ref[...]
    if o_scale != 1.0:
        on = on * o_scale
    half = 0.5 * gate_ref[...].astype(F32)
    y = on * (half + half * jnp.tanh(half))
    mix = _dot(y.astype(BF16), wout_ref[...])
    h1 = h_ref[...] + _rms(mix) * npost_ref[...]
    emb = _dot(p_ref[...].astype(BF16), wple_ref[...])
    egate = 0.5 + 0.5 * jnp.tanh(0.5 * _dot(_rms(h1).astype(BF16), wg_ref[...]))
    out_ref[...] = h1 + emb * egate


def _post(o, gate, h, p_all, layer, sub_gain, post_gains, wout16_all, mixer, wple16_all, wg16_all, head_dim, o_scale):
    m, d = h.shape
    tm = min(ROW_TILE, m)
    row = pl.BlockSpec((tm, d), lambda i: (i, 0))
    p_off = layer * (m // tm)
    return pl.pallas_call(
        functools.partial(_post_kernel, head_dim=head_dim, o_scale=o_scale),
        grid=(m // tm,),
        in_specs=[row, row, row, pl.BlockSpec((tm, p_all.shape[1]), lambda i: (i + p_off, 0)),
                  _full((1, d)), pl.BlockSpec((None, 1, d), lambda i: (layer, 0, 0)),
                  pl.BlockSpec((None,) + wout16_all.shape[1:], lambda i: (mixer, 0, 0)),
                  pl.BlockSpec((None,) + wple16_all.shape[1:], lambda i: (layer, 0, 0)),
                  pl.BlockSpec((None,) + wg16_all.shape[1:], lambda i: (layer, 0, 0))],
        out_specs=row,
        out_shape=jax.ShapeDtypeStruct((m, d), F32),
        compiler_params=_params(("parallel",)),
        name="post",
    )(o, gate, h, p_all, sub_gain, post_gains, wout16_all, wple16_all, wg16_all)


def _alibi_slope(h):
    return 2.0 ** (-8.0 * (h + 1) / A_HEADS)


def _lambda(lq1_ref, lk1_ref, lq2_ref, lk2_ref, lam_init):
    s1 = jnp.sum(lq1_ref[...] * lk1_ref[...], axis=-1, keepdims=True)
    s2 = jnp.sum(lq2_ref[...] * lk2_ref[...], axis=-1, keepdims=True)
    return jnp.exp(s1) - jnp.exp(s2) + lam_init


def _online_softmax_step(s, v16, m_ref, l_ref, acc_ref, idx):
    keys = s.shape[1]
    m_prev = m_ref[idx]
    m_next = jnp.maximum(m_prev, jnp.max(s, axis=1, keepdims=True))
    p = jnp.exp2(s - (pltpu.repeat(m_next, keys // LANE, 1) if keys > LANE else m_next[:, :keys]))
    alpha = jnp.exp2(m_prev - m_next)
    l_ref[idx] = alpha * l_ref[idx] + jnp.sum(p, axis=1, keepdims=True)
    acc_ref[idx] = pltpu.repeat(alpha, acc_ref.shape[-1] // LANE, 1) * acc_ref[idx] + _dot(p.astype(BF16), v16)
    m_ref[idx] = m_next


def _normalised(acc_ref, l_ref, idx):
    return acc_ref[idx] * pltpu.repeat(1.0 / l_ref[idx], acc_ref.shape[-1] // LANE, 1)


def _flash_body(h, i, q_ref, k_ref, v_ref, lam, o_ref, m_ref, l_ref, acc_ref, sa_ref, sb_ref, tile):
    slope = jnp.where(h == 0, _alibi_slope(0),
                      jnp.where(h == 1, _alibi_slope(1), jnp.where(h == 2, _alibi_slope(2), _alibi_slope(3))))
    m_ref[...] = jnp.full(m_ref.shape, MASK_VALUE, F32)
    l_ref[...] = jnp.zeros(l_ref.shape, F32)
    acc_ref[...] = jnp.zeros(acc_ref.shape, F32)
    col = lax.broadcasted_iota(jnp.int32, (1, tile), 1)

    def scores(j, s_ref):
        k16 = k_ref[pl.ds(pl.multiple_of(j * tile, tile), tile), :]
        bias = (slope * LOG2E) * (col + (j - i) * tile).astype(F32)
        for c in range(2):
            qk = _dot_nt(q_ref[:, c * A_HEAD_DIM:(c + 1) * A_HEAD_DIM], k16[:, c * A_HEAD_DIM:(c + 1) * A_HEAD_DIM])
            s_ref[c] = qk + bias

    def consume(j, s_ref, masked):
        v16 = v_ref[pl.ds(pl.multiple_of(j * tile, tile), tile), :]
        for c in range(2):
            s = s_ref[c]
            if masked:
                rows = lax.broadcasted_iota(jnp.int32, (tile, tile), 0)
                cols = lax.broadcasted_iota(jnp.int32, (tile, tile), 1)
                s = jnp.where(cols <= rows, s, MASK_VALUE)
            _online_softmax_step(s, v16, m_ref, l_ref, acc_ref, c)

    scores(0, sa_ref)

    def body(jj, carry):
        j = 2 * jj
        scores(j + 1, sb_ref)
        consume(j, sa_ref, False)
        scores(j + 2, sa_ref)
        consume(j + 1, sb_ref, False)
        return carry

    lax.fori_loop(0, i // 2, body, 0)

    @pl.when(i % 2 == 0)
    def _():
        consume(i, sa_ref, True)

    @pl.when(i % 2 == 1)
    def _():
        scores(i, sb_ref)
        consume(i - 1, sa_ref, False)
        consume(i, sb_ref, True)

    o_ref[...] = (_normalised(acc_ref, l_ref, 0) - lam * _normalised(acc_ref, l_ref, 1)).astype(o_ref.dtype)


def _paged_body(g, n_steps, q_ref, knew_ref, vnew_ref, k_refs, v_refs, lam, o_ref, m_ref, l_ref, acc_ref,
                page, past_len):
    n_group = len(k_refs)
    t = q_ref.shape[0]
    nrow = A_HEADS * t
    pkeys = page * A_HEADS

    @pl.when(g == 0)
    def _():
        m_ref[...] = jnp.full(m_ref.shape, MASK_VALUE, F32)
        l_ref[...] = jnp.zeros(l_ref.shape, F32)
        acc_ref[...] = jnp.zeros(acc_ref.shape, F32)

    def queries(c):
        parts = [q_ref[:, (2 * h + c) * A_HEAD_DIM:(2 * h + c + 1) * A_HEAD_DIM] for h in range(A_HEADS)]
        return jnp.concatenate(parts, axis=0).astype(BF16)

    row_head = lax.broadcasted_iota(jnp.int32, (nrow, 1), 0) // t
    slope = jnp.where(row_head == 0, _alibi_slope(0), jnp.where(row_head == 1, _alibi_slope(1),
                      jnp.where(row_head == 2, _alibi_slope(2), _alibi_slope(3)))) * LOG2E

    def attend(k16, v16, key_head, key_pos, extra_mask):
        valid = key_head == row_head
        if extra_mask is not None:
            valid = valid & extra_mask
        bias = slope * key_pos.astype(F32)
        for c in range(2):
            s = _dot_nt(queries(c), k16[:, c * A_HEAD_DIM:(c + 1) * A_HEAD_DIM])
            s = jnp.where(valid, s + bias, MASK_VALUE)
            _online_softmax_step(s, v16, m_ref, l_ref, acc_ref, c)

    for first in range(0, n_group, PAGES_PER_PASS):
        part = range(first, min(first + PAGES_PER_PASS, n_group))
        col = lax.broadcasted_iota(jnp.int32, (1, len(part) * pkeys), 1)
        k16 = jnp.concatenate([k_refs[r][...].reshape(pkeys, A_V_DIM) for r in part], axis=0).astype(BF16)
        v16 = jnp.concatenate([v_refs[r][...].reshape(pkeys, A_V_DIM) for r in part], axis=0).astype(BF16)
        attend(k16, v16, col % A_HEADS, (g * n_group + first) * page + col // A_HEADS - past_len, None)

    @pl.when(g == n_steps - 1)
    def _():
        pad = jnp.zeros((LANE - nrow, A_V_DIM), F32)
        kn = jnp.concatenate([knew_ref[...].reshape(nrow, A_V_DIM), pad], axis=0)
        vn = jnp.concatenate([vnew_ref[...].reshape(nrow, A_V_DIM), pad], axis=0)
        ncol = lax.broadcasted_iota(jnp.int32, (1, LANE), 1)
        row_tok = lax.broadcasted_iota(jnp.int32, (nrow, 1), 0) % t
        causal = (ncol // A_HEADS <= row_tok) & (ncol < nrow)
        attend(kn.astype(BF16), vn.astype(BF16), ncol % A_HEADS, ncol // A_HEADS, causal)
        o = _normalised(acc_ref, l_ref, 0) - lam * _normalised(acc_ref, l_ref, 1)
        for h in range(A_HEADS):
            o_ref[:, h * A_V_DIM:(h + 1) * A_V_DIM] = o[h * t:(h + 1) * t]


def _attention_kernel(pt_ref, q_ref, k_ref, v_ref, lq1_ref, lk1_ref, lq2_ref, lk2_ref, qs_ref, knew_ref, vnew_ref,
                      *rest, n_group, tile, page, past_len, lam_init):
    del pt_ref
    k_refs = rest[:n_group]
    v_refs = rest[n_group:2 * n_group]
    o_ref, os_ref, m_ref, l_ref, acc_ref, sa_ref, sb_ref, ms_ref, ls_ref, accs_ref = rest[2 * n_group:]
    b, h, i = pl.program_id(0), pl.program_id(1), pl.program_id(2)
    step = (b * pl.num_programs(1) + h) * pl.num_programs(2) + i
    n_steps = (past_len // page) // n_group
    lam = _lambda(lq1_ref, lk1_ref, lq2_ref, lk2_ref, lam_init)
    _paged_body(step % n_steps, n_steps, qs_ref, knew_ref, vnew_ref, k_refs, v_refs, lam, os_ref,
                ms_ref, ls_ref, accs_ref, page, past_len)
    _flash_body(h, i, q_ref, k_ref, v_ref, lam, o_ref, m_ref, l_ref, acc_ref, sa_ref, sb_ref, tile)


def _attention(q16, k16, v16, q32, k_new, v_new, new_off, cache_k, cache_v, page_rows, lam_refs, bsz, dec_b,
               lam_init):
    m, d = q16.shape
    seq = m // bsz
    tile = min(ATTN_TILE, seq)
    nq = seq // tile
    ms = q32.shape[0]
    t = ms // dec_b
    page = cache_k.shape[1]
    n_pages = page_rows.shape[0] // dec_b
    steps = bsz * A_HEADS * nq
    n_group = (dec_b * n_pages) // steps
    assert n_group * steps == dec_b * n_pages and n_group >= 1 and n_pages % n_group == 0, (steps, dec_b, n_pages)
    n_steps = n_pages // n_group
    nrow = A_HEADS * t

    def sample_seq(b, h, i):
        return ((b * A_HEADS + h) * nq + i) // n_steps

    qspec = pl.BlockSpec((tile, A_V_DIM), lambda b, h, i, pt: (b * nq + i, h))
    kvspec = pl.BlockSpec((seq, A_V_DIM), lambda b, h, i, pt: (b, h), pipeline_mode=pl.Buffered(1))
    lspec = pl.BlockSpec((1, A_HEAD_DIM), lambda b, h, i, pt: (0, 0))
    tok = pl.BlockSpec((t, d), lambda b, h, i, pt: (sample_seq(b, h, i), 0))
    new = pl.BlockSpec((t, A_HEADS, A_V_DIM), lambda b, h, i, pt: (sample_seq(b, h, i) + new_off, 0, 0))

    def page_spec(r):
        return pl.BlockSpec((None, page, A_HEADS, A_V_DIM),
                            lambda b, h, i, pt: (pt[((b * A_HEADS + h) * nq + i) * n_group + r], 0, 0, 0))

    pages = [page_spec(r) for r in range(n_group)]
    grid_spec = pltpu.PrefetchScalarGridSpec(
        num_scalar_prefetch=1,
        grid=(bsz, A_HEADS, nq),
        in_specs=[qspec, kvspec, kvspec, lspec, lspec, lspec, lspec, tok, new, new] + pages + pages,
        out_specs=(qspec, tok),
        scratch_shapes=[pltpu.VMEM((2, tile, LANE), F32), pltpu.VMEM((2, tile, LANE), F32),
                        pltpu.VMEM((2, tile, A_V_DIM), F32), pltpu.VMEM((2, tile, tile), F32),
                        pltpu.VMEM((2, tile, tile), F32),
                        pltpu.VMEM((2, nrow, LANE), F32), pltpu.VMEM((2, nrow, LANE), F32),
                        pltpu.VMEM((2, nrow, A_V_DIM), F32)],
    )
    return pl.pallas_call(
        functools.partial(_attention_kernel, n_group=n_group, tile=tile, page=page, past_len=n_pages * page,
                          lam_init=lam_init),
        grid_spec=grid_spec,
        out_shape=(jax.ShapeDtypeStruct((m, d), BF16), jax.ShapeDtypeStruct((ms, d), F32)),
        compiler_params=_params(("arbitrary", "arbitrary", "arbitrary")),
        name="attention",
    )(page_rows, q16, k16, v16, *lam_refs, q32, k_new, v_new, *([cache_k] * n_group), *([cache_v] * n_group))


def _mm(a, b, dims=None, cast=True):
    if cast:
        a, b = a.astype(BF16), b.astype(BF16)
    if dims is None:
        return jnp.dot(a, b, preferred_element_type=F32)
    return lax.dot_general(a, b, (dims, ((), ())), preferred_element_type=F32)


_NT = ((1,), (1,))
_TN = ((0,), (0,))


def _gdn_kernel(qkv_ref, ba_ref, convw_ref, alog_ref, dtb_ref, conv0_ref, s0_ref,
                o_ref, convout_ref, sout_ref, carry_ref, y_ref, s_ref, beta_ref, gc_ref, *, chunk):
    r = pl.program_id(1)
    rows, width = qkv_ref.shape[1], qkv_ref.shape[2]
    d = width // 3
    tail = CONV_WIDTH - 1
    n_chunks = rows // chunk
    hd = B_HEAD_DIM
    cast = chunk % 16 == 0

    @pl.when(r == 0)
    def _():
        carry_ref[...] = jnp.zeros(carry_ref.shape, F32)
        carry_ref[SUBLANE - tail:SUBLANE, :] = conv0_ref[0]
        s_ref[...] = s0_ref[0]

    x = qkv_ref[0]
    x3 = x.reshape(rows // SUBLANE, SUBLANE, width)
    prev = carry_ref[...]
    sub = lax.broadcasted_iota(jnp.int32, x3.shape, 1)
    conv = convw_ref[tail:tail + 1, :] * x3
    rolled, prev_rolled = x3, prev
    for shift in range(1, CONV_WIDTH):
        rolled = pltpu.roll(rolled, 1, 1)
        prev_rolled = pltpu.roll(prev_rolled, 1, 0)
        before = prev_rolled[None]
        if rows > SUBLANE:
            before = jnp.concatenate([before, rolled[:-1]], axis=0)
        conv = conv + convw_ref[tail - shift:tail - shift + 1, :] * jnp.where(sub < shift, before, rolled)
    conv = conv.reshape(rows, width)
    half = 0.5 * conv
    y_ref[...] = half + half * jnp.tanh(half)
    carry_ref[...] = x[rows - SUBLANE:rows]

    @pl.when(r == pl.num_programs(1) - 1)
    def _():
        convout_ref[0] = qkv_ref[0, rows - tail:rows, :]

    ba = ba_ref[0]
    beta_ref[...] = _sigmoid(ba)
    g_all = -jnp.exp(alog_ref[...]) * _softplus(ba + dtb_ref[...])
    ri = lax.broadcasted_iota(jnp.int32, (rows, rows), 0)
    rj = lax.broadcasted_iota(jnp.int32, (rows, rows), 1)
    block_tril = jnp.where(((ri // chunk) == (rj // chunk)) & (ri >= rj), 1.0, 0.0)
    gc_ref[...] = _dot_exact(block_tril, g_all)

    ii = lax.broadcasted_iota(jnp.int32, (chunk, chunk), 0)
    jj = lax.broadcasted_iota(jnp.int32, (chunk, chunk), 1)
    lower = ii >= jj
    strict = ii > jj
    eye = (ii == jj).astype(F32)
    n_squarings = max(int(math.log2(chunk)) - 1, 0)
    heads = range(B_HEADS)

    def hs(h, base=0):
        return slice(base + h * hd, base + (h + 1) * hd)

    def group_step(c0, n_sub):
        pairs = [(ci, h) for ci in range(n_sub) for h in heads]
        npairs = range(len(pairs))

        def rs(ci):
            return pl.ds(c0 + ci * chunk, chunk)

        beta_c = [beta_ref[rs(ci), :] for ci in range(n_sub)]
        gc_c = [gc_ref[rs(ci), :] for ci in range(n_sub)]
        qs, ks, kbs, rhs, decays, qgs, kts, egs = [], [], [], [], [], [], [], []
        for ci, h in pairs:
            q = y_ref[rs(ci), hs(h)]
            k = y_ref[rs(ci), hs(h, d)]
            v = y_ref[rs(ci), hs(h, 2 * d)]
            q = q * (lax.rsqrt(jnp.sum(q * q, axis=-1, keepdims=True) + NORM_EPS) * (hd ** -0.5))
            k = k * lax.rsqrt(jnp.sum(k * k, axis=-1, keepdims=True) + NORM_EPS)
            beta_b = jnp.broadcast_to(beta_c[ci][:, h:h + 1], (chunk, hd))
            gc_b = jnp.broadcast_to(gc_c[ci][:, B_HEADS + h:B_HEADS + h + 1], (chunk, hd))
            gc_row = gc_b.T[0:1, 0:chunk]
            g_last_b = gc_b[chunk - 1:chunk, :]
            egc_b = jnp.exp(gc_b)
            decays.append(jnp.where(lower, jnp.exp(jnp.where(lower, gc_b[:, :chunk] - gc_row, 0.0)), 0.0))
            kb = k * beta_b
            rhs.append(jnp.concatenate([v * beta_b, kb * egc_b], axis=-1))
            qgs.append(q * egc_b)
            kts.append(k * jnp.exp(g_last_b - gc_b))
            egs.append(jnp.exp(g_last_b))
            qs.append(q)
            ks.append(k)
            kbs.append(kb)
        kk = [_mm(kbs[p], ks[p], _NT, cast) for p in npairs]
        qk = [_mm(qs[p], ks[p], _NT, cast) for p in npairs]
        qk = [jnp.where(lower, qk[p] * decays[p], 0.0) for p in npairs]
        power = [jnp.where(strict, kk[p] * decays[p], 0.0) for p in npairs]
        inv = [eye - a for a in power]
        for _ in range(n_squarings):
            power = [_mm(a, a, None, cast) for a in power]
            upd = [_mm(inv[p], power[p], None, cast) for p in npairs]
            inv = [inv[p] + upd[p] for p in npairs]
        uw = [_mm(inv[p], rhs[p], None, cast) for p in npairs]
        wq = [jnp.concatenate([uw[p][:, hd:], qgs[p]], axis=0) for p in npairs]
        for ci in range(n_sub):
            base = ci * B_HEADS
            state = [s_ref[h] for h in heads]
            ws = [_mm(wq[base + h], state[h], None, cast) for h in heads]
            v_new = [uw[base + h][:, :hd] - ws[h][:chunk] for h in heads]
            intra = [_mm(qk[base + h], v_new[h], None, cast) for h in heads]
            for h in heads:
                o_ref[0, rs(ci), hs(h)] = (ws[h][chunk:] + intra[h]).astype(o_ref.dtype)
            upd = [_mm(kts[base + h], v_new[h], _TN, cast) for h in heads]
            for h in heads:
                s_ref[h] = state[h] * egs[base + h] + upd[h]

    group = math.gcd(n_chunks, GDN_GROUP)
    if n_chunks == group:
        group_step(0, group)
    else:
        def body(gi, carry):
            group_step(pl.multiple_of(gi * (group * chunk), group * chunk), group)
            return carry
        lax.fori_loop(0, n_chunks // group, body, 0)

    @pl.when(r == pl.num_programs(1) - 1)
    def _():
        sout_ref[0] = s_ref[...]


def _gdn(qkv, ba, conv_w, alog_row, dtb_row, conv0, s0, s0_off, rows):
    bsz, seq, width = qkv.shape
    d = width // 3
    chunk = math.gcd(seq, GDN_CHUNK)
    nr = seq // rows
    o_dtype = BF16 if chunk % 16 == 0 else F32
    out_shape = (jax.ShapeDtypeStruct((bsz, seq, d), o_dtype), jax.ShapeDtypeStruct(conv0.shape, F32),
                 jax.ShapeDtypeStruct((bsz,) + s0.shape[1:], F32))
    return pl.pallas_call(
        functools.partial(_gdn_kernel, chunk=chunk),
        grid=(bsz, nr),
        in_specs=[pl.BlockSpec((1, rows, width), lambda b, r: (b, r, 0)),
                  pl.BlockSpec((1, rows, LANE), lambda b, r: (b, r, 0)),
                  _full(conv_w.shape), _full(alog_row.shape), _full(dtb_row.shape),
                  pl.BlockSpec((1,) + conv0.shape[1:], lambda b, r: (b, 0, 0)),
                  pl.BlockSpec((1,) + s0.shape[1:], lambda b, r: (b + s0_off, 0, 0, 0))],
        out_specs=(pl.BlockSpec((1, rows, d), lambda b, r: (b, r, 0)),
                   pl.BlockSpec((1,) + conv0.shape[1:], lambda b, r: (b, 0, 0)),
                   pl.BlockSpec((1,) + s0.shape[1:], lambda b, r: (b, 0, 0, 0))),
        out_shape=out_shape,
        scratch_shapes=[pltpu.VMEM((SUBLANE, width), F32), pltpu.VMEM((rows, width), F32),
                        pltpu.VMEM(s0.shape[1:], F32), pltpu.VMEM((rows, LANE), F32),
                        pltpu.VMEM((rows, LANE), F32)],
        compiler_params=_params(("parallel", "arbitrary")),
        name="gdn",
    )(qkv, ba, conv_w, alog_row, dtb_row, conv0, s0)


def _lane_row(vec, offset):
    return jnp.zeros((1, LANE), F32).at[0, offset:offset + vec.shape[0]].set(vec.astype(F32))


def kernel(x_prompt, x_sample, cache_k, cache_v, state_conv, state_ssm, page_table, p_prompt, p_sample, norm_pre, norm_post, w_in_a, lambda_q1, lambda_k1, lambda_q2, lambda_k2, subln_a, w_out_a, w_in_b, conv_b, a_log_b, dt_bias_b, onorm_b, w_out_b, w_ple, w_ple_gate):
    bsz, seq, d = x_prompt.shape
    dec_b, dec_seq, _ = x_sample.shape
    depth = norm_pre.shape[0]
    n_phys = cache_k.shape[1]
    n_pages = page_table.shape[1]
    assert d == A_HEADS * A_V_DIM == B_HEADS * B_HEAD_DIM
    assert seq % GDN_ROWS == 0 and GDN_ROWS % GDN_CHUNK == 0

    hp = x_prompt.reshape(bsz * seq, d)
    hs = x_sample.reshape(dec_b * dec_seq, d)
    cache_k2 = cache_k.reshape((cache_k.shape[0] * n_phys,) + cache_k.shape[2:])
    cache_v2 = cache_v.reshape((cache_v.shape[0] * n_phys,) + cache_v.shape[2:])
    zero_conv = jnp.zeros((bsz, CONV_WIDTH - 1, 3 * d), F32)
    zero_ssm = jnp.zeros((bsz, B_HEADS, B_HEAD_DIM, B_HEAD_DIM), F32)
    ssm_all = state_ssm.reshape((state_ssm.shape[0] * dec_b,) + state_ssm.shape[2:])
    pp_all = p_prompt.reshape(depth * bsz * seq, -1)
    ps_all = p_sample.reshape(depth * dec_b * dec_seq, -1)

    n_attn = (depth + N_MIXERS - 1) // N_MIXERS
    pre_gains = norm_pre.reshape(depth, 1, d)
    post_gains = norm_post.reshape(depth, 1, d)
    w_in_a16 = w_in_a.astype(BF16)
    w_out_a16 = w_out_a.astype(BF16)
    w_out_b16 = w_out_b.astype(BF16)
    w_ple16 = w_ple.astype(BF16)
    w_gate16 = w_ple_gate.astype(BF16)

    kv_p, kv_s = None, None
    nc_p, ns_p, nc_s, ns_s = [], [], [], []
    for i in range(depth):
        j = i // N_MIXERS
        if i % N_MIXERS == 0:
            lam_init = 0.8 - 0.6 * math.exp(-0.3 * i)
            lam_refs = [t[j].reshape(1, A_HEAD_DIM) for t in (lambda_q1, lambda_k1, lambda_q2, lambda_k2)]
            sub_gain = jnp.tile(subln_a[j].reshape(1, A_V_DIM), (1, A_HEADS))
            q16, k_p, v_p, k16, v16, gate_p = _proj_attn(hp, pre_gains, i, w_in_a16, j, n_attn, kv_p, BF16)
            kv_p = (k_p, v_p)
            q32, k_s, v_s, _, _, gate_s = _proj_attn(hs, pre_gains, i, w_in_a16, j, n_attn, kv_s, F32)
            kv_s = (k_s, v_s)
            page_rows = (page_table + j * n_phys).reshape(dec_b * n_pages)
            o_p, o_s = _attention(q16, k16, v16, q32, k_s.reshape((-1,) + k_s.shape[2:]),
                                  v_s.reshape((-1,) + v_s.shape[2:]), j * dec_b, cache_k2, cache_v2, page_rows,
                                  lam_refs, bsz, dec_b, lam_init)
            wout16_all, head_dim, o_scale = w_out_a16, A_V_DIM, 1.0 - lam_init
        else:
            w16 = w_in_b[j][:, :4 * d].astype(BF16)
            wba = w_in_b[j][:, 4 * d:]
            wba16 = jnp.zeros((d, LANE), F32).at[:, :2 * B_HEADS].set(wba).astype(BF16)
            sub_gain = jnp.tile(onorm_b[j].reshape(1, B_HEAD_DIM), (1, B_HEADS))
            alog_row = _lane_row(a_log_b[j], B_HEADS)
            dtb_row = _lane_row(dt_bias_b[j], B_HEADS)
            qkv, gate_p, ba = _proj_gdn(hp, pre_gains, i, w16, wba16)
            o_p, cbuf, sst = _gdn(qkv.reshape(bsz, seq, 3 * d), ba.reshape(bsz, seq, LANE), conv_b[j],
                                  alog_row, dtb_row, zero_conv, zero_ssm, 0, GDN_ROWS)
            o_p = o_p.reshape(bsz * seq, d)
            nc_p.append(cbuf)
            ns_p.append(sst)
            qkv, gate_s, ba = _proj_gdn(hs, pre_gains, i, w16, wba16)
            o_s, cbuf, sst = _gdn(qkv.reshape(dec_b, dec_seq, 3 * d), ba.reshape(dec_b, dec_seq, LANE), conv_b[j],
                                  alog_row, dtb_row, state_conv[j], ssm_all, j * dec_b, dec_seq)
            o_s = o_s.reshape(dec_b * dec_seq, d)
            nc_s.append(cbuf)
            ns_s.append(sst)
            wout16_all, head_dim, o_scale = w_out_b16, B_HEAD_DIM, 1.0
        hp = _post(o_p, gate_p, hp, pp_all, i, sub_gain, post_gains, wout16_all, j, w_ple16, w_gate16, head_dim, o_scale)
        hs = _post(o_s, gate_s, hs, ps_all, i, sub_gain, post_gains, wout16_all, j, w_ple16, w_gate16, head_dim, o_scale)

    def per_seq(stacked, n):
        return stacked.reshape((stacked.shape[0], n, -1) + stacked.shape[2:])

    return (hp.reshape(bsz, seq, d), hs.reshape(dec_b, dec_seq, d),
            per_seq(kv_p[0], bsz), per_seq(kv_p[1], bsz), jnp.stack(nc_p), jnp.stack(ns_p),
            per_seq(kv_s[0], dec_b), per_seq(kv_s[1], dec_b), jnp.stack(nc_s), jnp.stack(ns_s))
```

```python
import functools
import math

import jax
import jax.numpy as jnp
from jax import lax
from jax.experimental import pallas as pl
from jax.experimental.pallas import tpu as pltpu

F32 = jnp.float32
BF16 = jnp.bfloat16

A_HEADS = 4
A_HEAD_DIM = 128
A_V_DIM = 2 * A_HEAD_DIM
B_HEADS = 8
B_HEAD_DIM = 128
CONV_WIDTH = 4
GDN_CHUNK = 64
NORM_EPS = 1e-6
N_MIXERS = 2

LANE = 128
SUBLANE = 8
V7X_VMEM_BYTES = 64 * 1024 * 1024
VMEM_LIMIT = V7X_VMEM_BYTES - 8 * 1024 * 1024

MASK_VALUE = -1e30
LOG2E = math.log2(math.e)

ROW_TILE = 512
ATTN_TILE = 512
GDN_ROWS = 256
PAGES_PER_PASS = 2
POST_SPLIT = 4
GDN_GROUP = 4


def _params(semantics):
    return pltpu.CompilerParams(dimension_semantics=semantics, vmem_limit_bytes=VMEM_LIMIT)


def _full(shape):
    zeros = (0,) * len(shape)
    return pl.BlockSpec(shape, lambda *_: zeros)


def _rms(x):
    return x * lax.rsqrt(jnp.mean(x * x, axis=-1, keepdims=True) + NORM_EPS)


def _softplus(x):
    return jnp.maximum(x, 0.0) + jnp.log1p(jnp.exp(-jnp.abs(x)))


def _sigmoid(x):
    return 1.0 / (1.0 + jnp.exp(-x))


def _dot(a, b):
    return jnp.dot(a, b, preferred_element_type=F32)


def _dot_nt(a, b):
    return lax.dot_general(a, b, (((1,), (1,)), ((), ())), preferred_element_type=F32)


def _dot_exact(a, b):
    return jnp.dot(a, b, preferred_element_type=F32, precision=lax.Precision.HIGHEST)


def _proj_attn_kernel(x_ref, g_ref, w_ref, *rest, q_scale, slot, n_slots, first):
    if first:
        q_ref, k_ref, v_ref, k16_ref, v16_ref, gate_ref = rest
    else:
        q_ref, k_ref, v_ref, k16_ref, v16_ref, gate_ref = rest[2:]
    d = x_ref.shape[1]
    xn = (_rms(x_ref[...]) * g_ref[...]).astype(BF16)
    q_ref[...] = (_dot(xn, w_ref[:, 0:d]) * q_scale).astype(q_ref.dtype)
    k = _dot(xn, w_ref[:, d:2 * d])
    k16_ref[...] = k.astype(BF16)
    v = _dot(xn, w_ref[:, 2 * d:3 * d])
    v16_ref[...] = v.astype(BF16)
    for h in range(A_HEADS):
        if first:
            k_ref[slot, :, h, :] = k[:, h * A_V_DIM:(h + 1) * A_V_DIM]
            v_ref[slot, :, h, :] = v[:, h * A_V_DIM:(h + 1) * A_V_DIM]
        else:
            k_ref[:, h, :] = k[:, h * A_V_DIM:(h + 1) * A_V_DIM]
            v_ref[:, h, :] = v[:, h * A_V_DIM:(h + 1) * A_V_DIM]
    if first:
        for other in range(n_slots):
            if other != slot:
                k_ref[other] = jnp.zeros(k_ref.shape[1:], F32)
                v_ref[other] = jnp.zeros(v_ref.shape[1:], F32)
    gate_ref[...] = _dot(xn, w_ref[:, 3 * d:4 * d]).astype(gate_ref.dtype)


def _proj_attn(x, gains, layer, w16_all, slot, n_slots, kv_prev, q_dtype):
    m, d = x.shape
    tm = min(ROW_TILE, m)
    first = kv_prev is None
    row = pl.BlockSpec((tm, d), lambda i: (i, 0))
    gain_spec = pl.BlockSpec((None, 1, d), lambda i: (layer, 0, 0))
    w_spec = pl.BlockSpec((None,) + w16_all.shape[1:], lambda i: (slot, 0, 0))
    if first:
        heads = pl.BlockSpec((n_slots, tm, A_HEADS, A_V_DIM), lambda i: (0, i, 0, 0))
        extra_in, extra_specs, aliases = (), [], {}
    else:
        heads = pl.BlockSpec((None, tm, A_HEADS, A_V_DIM), lambda i: (slot, i, 0, 0))
        extra_in, extra_specs = tuple(kv_prev), [pl.BlockSpec(memory_space=pl.ANY)] * 2
        aliases = {3: 1, 4: 2}
    stacked = jax.ShapeDtypeStruct((n_slots, m, A_HEADS, A_V_DIM), F32)
    out_shape = (jax.ShapeDtypeStruct((m, d), q_dtype), stacked, stacked, jax.ShapeDtypeStruct((m, d), BF16),
                 jax.ShapeDtypeStruct((m, d), BF16), jax.ShapeDtypeStruct((m, d), BF16))
    return pl.pallas_call(
        functools.partial(_proj_attn_kernel, q_scale=A_HEAD_DIM ** -0.5 * LOG2E, slot=slot, n_slots=n_slots,
                          first=first),
        grid=(m // tm,),
        in_specs=[row, gain_spec, w_spec] + extra_specs,
        out_specs=(row, heads, heads, row, row, row),
        out_shape=out_shape,
        input_output_aliases=aliases,
        compiler_params=_params(("parallel",)),
        name="proj_attn",
    )(x, gains, w16_all, *extra_in)


def _proj_gdn_kernel(x_ref, g_ref, w_ref, wba_ref, qkv_ref, z_ref, ba_ref):
    d = x_ref.shape[1]
    xn = (_rms(x_ref[...]) * g_ref[...]).astype(BF16)
    for j in range(3):
        qkv_ref[:, j * d:(j + 1) * d] = _dot(xn, w_ref[:, j * d:(j + 1) * d])
    z_ref[...] = _dot(xn, w_ref[:, 3 * d:4 * d]).astype(z_ref.dtype)
    ba_ref[...] = _dot(xn, wba_ref[...])


def _proj_gdn(x, gains, layer, w16, wba16):
    m, d = x.shape
    tm = min(ROW_TILE, m)
    row = pl.BlockSpec((tm, d), lambda i: (i, 0))
    out_shape = (jax.ShapeDtypeStruct((m, 3 * d), F32), jax.ShapeDtypeStruct((m, d), BF16),
                 jax.ShapeDtypeStruct((m, LANE), F32))
    return pl.pallas_call(
        _proj_gdn_kernel,
        grid=(m // tm,),
        in_specs=[row, pl.BlockSpec((None, 1, d), lambda i: (layer, 0, 0)), _full(w16.shape), _full(wba16.shape)],
        out_specs=(pl.BlockSpec((tm, 3 * d), lambda i: (i, 0)), row, pl.BlockSpec((tm, LANE), lambda i: (i, 0))),
        out_shape=out_shape,
        compiler_params=_params(("parallel",)),
        name="proj_gdn",
    )(x, gains, w16, wba16)


def _post_kernel(o_ref, gate_ref, h_ref, p_ref, sub_ref, npost_ref, wout_ref, wple_ref, wg_ref, out_ref,
                 *, head_dim, o_scale):
    tm, d = o_ref.shape
    n_split = POST_SPLIT if tm % (POST_SPLIT * 16) == 0 else 1
    groups = [slice(r * (tm // n_split), (r + 1) * (tm // n_split)) for r in range(n_split)]
    ys = []
    for rows in groups:
        o = o_ref[rows, :].astype(F32)
        parts = [_rms(o[:, s:s + head_dim]) for s in range(0, d, head_dim)]
        on = jnp.concatenate(parts, axis=-1) * sub_ref[...]
        if o_scale != 1.0:
            on = on * o_scale
        half = 0.5 * gate_ref[rows, :].astype(F32)
        ys.append((on * (half + half * jnp.tanh(half))).astype(BF16))
    mixes = [_dot(y, wout_ref[...]) for y in ys]
    h1s = [h_ref[rows, :] + _rms(mix) * npost_ref[...] for rows, mix in zip(groups, mixes)]
    embs = [_dot(p_ref[rows, :].astype(BF16), wple_ref[...]) for rows in groups]
    gates = [_dot(_rms(h1).astype(BF16), wg_ref[...]) for h1 in h1s]
    for rows, h1, emb, gate in zip(groups, h1s, embs, gates):
        out_ref[rows, :] = h1 + emb * (0.5 + 0.5 * jnp.tanh(0.5 * gate))


def _post(o, gate, h, p_all, layer, sub_gain, post_gains, wout16_all, mixer, wple16_all, wg16_all, head_dim, o_scale):
    m, d = h.shape
    tm = min(ROW_TILE, m)
    row = pl.BlockSpec((tm, d), lambda i: (i, 0))
    p_off = layer * (m // tm)
    return pl.pallas_call(
        functools.partial(_post_kernel, head_dim=head_dim, o_scale=o_scale),
        grid=(m // tm,),
        in_specs=[row, row, row, pl.BlockSpec((tm, p_all.shape[1]), lambda i: (i + p_off, 0)),
                  _full((1, d)), pl.BlockSpec((None, 1, d), lambda i: (layer, 0, 0)),
                  pl.BlockSpec((None,) + wout16_all.shape[1:], lambda i: (mixer, 0, 0)),
                  pl.BlockSpec((None,) + wple16_all.shape[1:], lambda i: (layer, 0, 0)),
                  pl.BlockSpec((None,) + wg16_all.shape[1:], lambda i: (layer, 0, 0))],
        out_specs=row,
        out_shape=jax.ShapeDtypeStruct((m, d), F32),
        compiler_params=_params(("parallel",)),
        name="post",
    )(o, gate, h, p_all, sub_gain, post_gains, wout16_all, wple16_all, wg16_all)


def _alibi_slope(h):
    return 2.0 ** (-8.0 * (h + 1) / A_HEADS)


def _lambda(lq1_ref, lk1_ref, lq2_ref, lk2_ref, lam_init):
    s1 = jnp.sum(lq1_ref[...] * lk1_ref[...], axis=-1, keepdims=True)
    s2 = jnp.sum(lq2_ref[...] * lk2_ref[...], axis=-1, keepdims=True)
    return jnp.exp(s1) - jnp.exp(s2) + lam_init


def _online_softmax_step(s, v16, m_ref, l_ref, acc_ref, idx):
    keys = s.shape[1]
    m_prev = m_ref[idx]
    m_next = jnp.maximum(m_prev, jnp.max(s, axis=1, keepdims=True))
    p = jnp.exp2(s - (pltpu.repeat(m_next, keys // LANE, 1) if keys > LANE else m_next[:, :keys]))
    alpha = jnp.exp2(m_prev - m_next)
    l_ref[idx] = alpha * l_ref[idx] + jnp.sum(p, axis=1, keepdims=True)
    acc_ref[idx] = pltpu.repeat(alpha, acc_ref.shape[-1] // LANE, 1) * acc_ref[idx] + _dot(p.astype(BF16), v16)
    m_ref[idx] = m_next


def _normalised(acc_ref, l_ref, idx):
    return acc_ref[idx] * pltpu.repeat(1.0 / l_ref[idx], acc_ref.shape[-1] // LANE, 1)


def _flash_body(h, i, q_ref, k_ref, v_ref, lam, o_ref, m_ref, l_ref, acc_ref, sa_ref, sb_ref, tile):
    slope = jnp.where(h == 0, _alibi_slope(0),
                      jnp.where(h == 1, _alibi_slope(1), jnp.where(h == 2, _alibi_slope(2), _alibi_slope(3))))
    m_ref[...] = jnp.full(m_ref.shape, MASK_VALUE, F32)
    l_ref[...] = jnp.zeros(l_ref.shape, F32)
    acc_ref[...] = jnp.zeros(acc_ref.shape, F32)
    col = lax.broadcasted_iota(jnp.int32, (1, tile), 1)

    def scores(j, s_ref):
        k16 = k_ref[pl.ds(pl.multiple_of(j * tile, tile), tile), :]
        bias = (slope * LOG2E) * (col + (j - i) * tile).astype(F32)
        for c in range(2):
            qk = _dot_nt(q_ref[:, c * A_HEAD_DIM:(c + 1) * A_HEAD_DIM], k16[:, c * A_HEAD_DIM:(c + 1) * A_HEAD_DIM])
            s_ref[c] = qk + bias

    def consume(j, s_ref, masked):
        v16 = v_ref[pl.ds(pl.multiple_of(j * tile, tile), tile), :]
        for c in range(2):
            s = s_ref[c]
            if masked:
                rows = lax.broadcasted_iota(jnp.int32, (tile, tile), 0)
                cols = lax.broadcasted_iota(jnp.int32, (tile, tile), 1)
                s = jnp.where(cols <= rows, s, MASK_VALUE)
            _online_softmax_step(s, v16, m_ref, l_ref, acc_ref, c)

    scores(0, sa_ref)

    def body(jj, carry):
        j = 2 * jj
        scores(j + 1, sb_ref)
        consume(j, sa_ref, False)
        scores(j + 2, sa_ref)
        consume(j + 1, sb_ref, False)
        return carry

    lax.fori_loop(0, i // 2, body, 0)

    @pl.when(i % 2 == 0)
    def _():
        consume(i, sa_ref, True)

    @pl.when(i % 2 == 1)
    def _():
        scores(i, sb_ref)
        consume(i - 1, sa_ref, False)
        consume(i, sb_ref, True)

    o_ref[...] = (_normalised(acc_ref, l_ref, 0) - lam * _normalised(acc_ref, l_ref, 1)).astype(o_ref.dtype)


def _paged_body(g, n_steps, q_ref, knew_ref, vnew_ref, k_refs, v_refs, lam, o_ref, m_ref, l_ref, acc_ref,
                page, past_len):
    n_group = len(k_refs)
    t = q_ref.shape[0]
    nrow = A_HEADS * t
    pkeys = page * A_HEADS

    @pl.when(g == 0)
    def _():
        m_ref[...] = jnp.full(m_ref.shape, MASK_VALUE, F32)
        l_ref[...] = jnp.zeros(l_ref.shape, F32)
        acc_ref[...] = jnp.zeros(acc_ref.shape, F32)

    def queries(c):
        parts = [q_ref[:, (2 * h + c) * A_HEAD_DIM:(2 * h + c + 1) * A_HEAD_DIM] for h in range(A_HEADS)]
        return jnp.concatenate(parts, axis=0).astype(BF16)

    row_head = lax.broadcasted_iota(jnp.int32, (nrow, 1), 0) // t
    slope = jnp.where(row_head == 0, _alibi_slope(0), jnp.where(row_head == 1, _alibi_slope(1),
                      jnp.where(row_head == 2, _alibi_slope(2), _alibi_slope(3)))) * LOG2E

    def attend(k16, v16, key_head, key_pos, extra_mask):
        valid = key_head == row_head
        if extra_mask is not None:
            valid = valid & extra_mask
        bias = slope * key_pos.astype(F32)
        for c in range(2):
            s = _dot_nt(queries(c), k16[:, c * A_HEAD_DIM:(c + 1) * A_HEAD_DIM])
            s = jnp.where(valid, s + bias, MASK_VALUE)
            _online_softmax_step(s, v16, m_ref, l_ref, acc_ref, c)

    for first in range(0, n_group, PAGES_PER_PASS):
        part = range(first, min(first + PAGES_PER_PASS, n_group))
        col = lax.broadcasted_iota(jnp.int32, (1, len(part) * pkeys), 1)
        k16 = jnp.concatenate([k_refs[r][...].reshape(pkeys, A_V_DIM) for r in part], axis=0).astype(BF16)
        v16 = jnp.concatenate([v_refs[r][...].reshape(pkeys, A_V_DIM) for r in part], axis=0).astype(BF16)
        attend(k16, v16, col % A_HEADS, (g * n_group + first) * page + col // A_HEADS - past_len, None)

    @pl.when(g == n_steps - 1)
    def _():
        pad = jnp.zeros((LANE - nrow, A_V_DIM), F32)
        kn = jnp.concatenate([knew_ref[...].reshape(nrow, A_V_DIM), pad], axis=0)
        vn = jnp.concatenate([vnew_ref[...].reshape(nrow, A_V_DIM), pad], axis=0)
        ncol = lax.broadcasted_iota(jnp.int32, (1, LANE), 1)
        row_tok = lax.broadcasted_iota(jnp.int32, (nrow, 1), 0) % t
        causal = (ncol // A_HEADS <= row_tok) & (ncol < nrow)
        attend(kn.astype(BF16), vn.astype(BF16), ncol % A_HEADS, ncol // A_HEADS, causal)
        o = _normalised(acc_ref, l_ref, 0) - lam * _normalised(acc_ref, l_ref, 1)
        for h in range(A_HEADS):
            o_ref[:, h * A_V_DIM:(h + 1) * A_V_DIM] = o[h * t:(h + 1) * t]


def _attention_kernel(pt_ref, q_ref, k_ref, v_ref, lq1_ref, lk1_ref, lq2_ref, lk2_ref, qs_ref, knew_ref, vnew_ref,
                      *rest, n_group, tile, page, past_len, lam_init):
    del pt_ref
    k_refs = rest[:n_group]
    v_refs = rest[n_group:2 * n_group]
    o_ref, os_ref, m_ref, l_ref, acc_ref, sa_ref, sb_ref, ms_ref, ls_ref, accs_ref = rest[2 * n_group:]
    b, h, i = pl.program_id(0), pl.program_id(1), pl.program_id(2)
    step = (b * pl.num_programs(1) + h) * pl.num_programs(2) + i
    n_steps = (past_len // page) // n_group
    lam = _lambda(lq1_ref, lk1_ref, lq2_ref, lk2_ref, lam_init)
    _paged_body(step % n_steps, n_steps, qs_ref, knew_ref, vnew_ref, k_refs, v_refs, lam, os_ref,
                ms_ref, ls_ref, accs_ref, page, past_len)
    _flash_body(h, i, q_ref, k_ref, v_ref, lam, o_ref, m_ref, l_ref, acc_ref, sa_ref, sb_ref, tile)


def _attention(q16, k16, v16, q32, k_new, v_new, new_off, cache_k, cache_v, page_rows, lam_refs, bsz, dec_b,
               lam_init):
    m, d = q16.shape
    seq = m // bsz
    tile = min(ATTN_TILE, seq)
    nq = seq // tile
    ms = q32.shape[0]
    t = ms // dec_b
    page = cache_k.shape[1]
    n_pages = page_rows.shape[0] // dec_b
    steps = bsz * A_HEADS * nq
    n_group = (dec_b * n_pages) // steps
    assert n_group * steps == dec_b * n_pages and n_group >= 1 and n_pages % n_group == 0, (steps, dec_b, n_pages)
    n_steps = n_pages // n_group
    nrow = A_HEADS * t

    def sample_seq(b, h, i):
        return ((b * A_HEADS + h) * nq + i) // n_steps

    qspec = pl.BlockSpec((tile, A_V_DIM), lambda b, h, i, pt: (b * nq + i, h))
    kvspec = pl.BlockSpec((seq, A_V_DIM), lambda b, h, i, pt: (b, h), pipeline_mode=pl.Buffered(1))
    lspec = pl.BlockSpec((1, A_HEAD_DIM), lambda b, h, i, pt: (0, 0))
    tok = pl.BlockSpec((t, d), lambda b, h, i, pt: (sample_seq(b, h, i), 0))
    new = pl.BlockSpec((t, A_HEADS, A_V_DIM), lambda b, h, i, pt: (sample_seq(b, h, i) + new_off, 0, 0))

    def page_spec(r):
        return pl.BlockSpec((None, page, A_HEADS, A_V_DIM),
                            lambda b, h, i, pt: (pt[((b * A_HEADS + h) * nq + i) * n_group + r], 0, 0, 0))

    pages = [page_spec(r) for r in range(n_group)]
    grid_spec = pltpu.PrefetchScalarGridSpec(
        num_scalar_prefetch=1,
        grid=(bsz, A_HEADS, nq),
        in_specs=[qspec, kvspec, kvspec, lspec, lspec, lspec, lspec, tok, new, new] + pages + pages,
        out_specs=(qspec, tok),
        scratch_shapes=[pltpu.VMEM((2, tile, LANE), F32), pltpu.VMEM((2, tile, LANE), F32),
                        pltpu.VMEM((2, tile, A_V_DIM), F32), pltpu.VMEM((2, tile, tile), F32),
                        pltpu.VMEM((2, tile, tile), F32),
                        pltpu.VMEM((2, nrow, LANE), F32), pltpu.VMEM((2, nrow, LANE), F32),
                        pltpu.VMEM((2, nrow, A_V_DIM), F32)],
    )
    return pl.pallas_call(
        functools.partial(_attention_kernel, n_group=n_group, tile=tile, page=page, past_len=n_pages * page,
                          lam_init=lam_init),
        grid_spec=grid_spec,
        out_shape=(jax.ShapeDtypeStruct((m, d), BF16), jax.ShapeDtypeStruct((ms, d), F32)),
        compiler_params=_params(("arbitrary", "arbitrary", "arbitrary")),
        name="attention",
    )(page_rows, q16, k16, v16, *lam_refs, q32, k_new, v_new, *([cache_k] * n_group), *([cache_v] * n_group))


def _mm(a, b, dims=None, cast=True):
    if cast:
        a, b = a.astype(BF16), b.astype(BF16)
    if dims is None:
        return jnp.dot(a, b, preferred_element_type=F32)
    return lax.dot_general(a, b, (dims, ((), ())), preferred_element_type=F32)


_NT = ((1,), (1,))
_TN = ((0,), (0,))


def _gdn_kernel(qkv_ref, ba_ref, convw_ref, alog_ref, dtb_ref, conv0_ref, s0_ref,
                o_ref, convout_ref, sout_ref, carry_ref, y_ref, s_ref, beta_ref, gc_ref, *, chunk):
    r = pl.program_id(1)
    rows, width = qkv_ref.shape[1], qkv_ref.shape[2]
    d = width // 3
    tail = CONV_WIDTH - 1
    n_chunks = rows // chunk
    hd = B_HEAD_DIM
    cast = chunk % 16 == 0

    @pl.when(r == 0)
    def _():
        carry_ref[...] = jnp.zeros(carry_ref.shape, F32)
        carry_ref[SUBLANE - tail:SUBLANE, :] = conv0_ref[0]
        s_ref[...] = s0_ref[0]

    x = qkv_ref[0]
    x3 = x.reshape(rows // SUBLANE, SUBLANE, width)
    prev = carry_ref[...]
    sub = lax.broadcasted_iota(jnp.int32, x3.shape, 1)
    conv = convw_ref[tail:tail + 1, :] * x3
    rolled, prev_rolled = x3, prev
    for shift in range(1, CONV_WIDTH):
        rolled = pltpu.roll(rolled, 1, 1)
        prev_rolled = pltpu.roll(prev_rolled, 1, 0)
        before = prev_rolled[None]
        if rows > SUBLANE:
            before = jnp.concatenate([before, rolled[:-1]], axis=0)
        conv = conv + convw_ref[tail - shift:tail - shift + 1, :] * jnp.where(sub < shift, before, rolled)
    conv = conv.reshape(rows, width)
    half = 0.5 * conv
    y_ref[...] = half + half * jnp.tanh(half)
    carry_ref[...] = x[rows - SUBLANE:rows]

    @pl.when(r == pl.num_programs(1) - 1)
    def _():
        convout_ref[0] = qkv_ref[0, rows - tail:rows, :]

    ba = ba_ref[0]
    beta_ref[...] = _sigmoid(ba)
    g_all = -jnp.exp(alog_ref[...]) * _softplus(ba + dtb_ref[...])
    ri = lax.broadcasted_iota(jnp.int32, (rows, rows), 0)
    rj = lax.broadcasted_iota(jnp.int32, (rows, rows), 1)
    block_tril = jnp.where(((ri // chunk) == (rj // chunk)) & (ri >= rj), 1.0, 0.0)
    gc_ref[...] = _dot_exact(block_tril, g_all)

    ii = lax.broadcasted_iota(jnp.int32, (chunk, chunk), 0)
    jj = lax.broadcasted_iota(jnp.int32, (chunk, chunk), 1)
    lower = ii >= jj
    strict = ii > jj
    eye = (ii == jj).astype(F32)
    n_squarings = max(int(math.log2(chunk)) - 1, 0)
    heads = range(B_HEADS)

    def hs(h, base=0):
        return slice(base + h * hd, base + (h + 1) * hd)

    def group_step(c0, n_sub):
        pairs = [(ci, h) for ci in range(n_sub) for h in heads]
        npairs = range(len(pairs))

        def rs(ci):
            return pl.ds(c0 + ci * chunk, chunk)

        beta_c = [beta_ref[rs(ci), :] for ci in range(n_sub)]
        gc_c = [gc_ref[rs(ci), :] for ci in range(n_sub)]
        qs, ks, kbs, rhs, decays, qgs, kts, egs = [], [], [], [], [], [], [], []
        for ci, h in pairs:
            q = y_ref[rs(ci), hs(h)]
            k = y_ref[rs(ci), hs(h, d)]
            v = y_ref[rs(ci), hs(h, 2 * d)]
            q = q * (lax.rsqrt(jnp.sum(q * q, axis=-1, keepdims=True) + NORM_EPS) * (hd ** -0.5))
            k = k * lax.rsqrt(jnp.sum(k * k, axis=-1, keepdims=True) + NORM_EPS)
            beta_b = jnp.broadcast_to(beta_c[ci][:, h:h + 1], (chunk, hd))
            gc_b = jnp.broadcast_to(gc_c[ci][:, B_HEADS + h:B_HEADS + h + 1], (chunk, hd))
            gc_row = gc_b.T[0:1, 0:chunk]
            g_last_b = gc_b[chunk - 1:chunk, :]
            egc_b = jnp.exp(gc_b)
            decays.append(jnp.where(lower, jnp.exp(jnp.where(lower, gc_b[:, :chunk] - gc_row, 0.0)), 0.0))
            kb = k * beta_b
            rhs.append(jnp.concatenate([v * beta_b, kb * egc_b], axis=-1))
            qgs.append(q * egc_b)
            kts.append(k * jnp.exp(g_last_b - gc_b))
            egs.append(jnp.exp(g_last_b))
            qs.append(q)
            ks.append(k)
            kbs.append(kb)
        kk = [_mm(kbs[p], ks[p], _NT, cast) for p in npairs]
        qk = [_mm(qs[p], ks[p], _NT, cast) for p in npairs]
        qk = [jnp.where(lower, qk[p] * decays[p], 0.0) for p in npairs]
        power = [jnp.where(strict, kk[p] * decays[p], 0.0) for p in npairs]
        inv = [eye - a for a in power]
        for _ in range(n_squarings):
            power = [_mm(a, a, None, cast) for a in power]
            upd = [_mm(inv[p], power[p], None, cast) for p in npairs]
            inv = [inv[p] + upd[p] for p in npairs]
        uw = [_mm(inv[p], rhs[p], None, cast) for p in npairs]
        wq = [jnp.concatenate([uw[p][:, hd:], qgs[p]], axis=0) for p in npairs]
        for ci in range(n_sub):
            base = ci * B_HEADS
            state = [s_ref[h] for h in heads]
            ws = [_mm(wq[base + h], state[h], None, cast) for h in heads]
            v_new = [uw[base + h][:, :hd] - ws[h][:chunk] for h in heads]
            intra = [_mm(qk[base + h], v_new[h], None, cast) for h in heads]
            for h in heads:
                o_ref[0, rs(ci), hs(h)] = (ws[h][chunk:] + intra[h]).astype(o_ref.dtype)
            upd = [_mm(kts[base + h], v_new[h], _TN, cast) for h in heads]
            for h in heads:
                s_ref[h] = state[h] * egs[base + h] + upd[h]

    group = math.gcd(n_chunks, GDN_GROUP)
    if n_chunks == group:
        group_step(0, group)
    else:
        def body(gi, carry):
            group_step(pl.multiple_of(gi * (group * chunk), group * chunk), group)
            return carry
        lax.fori_loop(0, n_chunks // group, body, 0)

    @pl.when(r == pl.num_programs(1) - 1)
    def _():
        sout_ref[0] = s_ref[...]


def _gdn(qkv, ba, conv_w, alog_row, dtb_row, conv0, s0, s0_off, rows):
    bsz, seq, width = qkv.shape
    d = width // 3
    chunk = math.gcd(seq, GDN_CHUNK)
    nr = seq // rows
    o_dtype = BF16 if chunk % 16 == 0 else F32
    out_shape = (jax.ShapeDtypeStruct((bsz, seq, d), o_dtype), jax.ShapeDtypeStruct(conv0.shape, F32),
                 jax.ShapeDtypeStruct((bsz,) + s0.shape[1:], F32))
    return pl.pallas_call(
        functools.partial(_gdn_kernel, chunk=chunk),
        grid=(bsz, nr),
        in_specs=[pl.BlockSpec((1, rows, width), lambda b, r: (b, r, 0)),
                  pl.BlockSpec((1, rows, LANE), lambda b, r: (b, r, 0)),
                  _full(conv_w.shape), _full(alog_row.shape), _full(dtb_row.shape),
                  pl.BlockSpec((1,) + conv0.shape[1:], lambda b, r: (b, 0, 0)),
                  pl.BlockSpec((1,) + s0.shape[1:], lambda b, r: (b + s0_off, 0, 0, 0))],
        out_specs=(pl.BlockSpec((1, rows, d), lambda b, r: (b, r, 0)),
                   pl.BlockSpec((1,) + conv0.shape[1:], lambda b, r: (b, 0, 0)),
                   pl.BlockSpec((1,) + s0.shape[1:], lambda b, r: (b, 0, 0, 0))),
        out_shape=out_shape,
        scratch_shapes=[pltpu.VMEM((SUBLANE, width), F32), pltpu.VMEM((rows, width), F32),
                        pltpu.VMEM(s0.shape[1:], F32), pltpu.VMEM((rows, LANE), F32),
                        pltpu.VMEM((rows, LANE), F32)],
        compiler_params=_params(("parallel", "arbitrary")),
        name="gdn",
    )(qkv, ba, conv_w, alog_row, dtb_row, conv0, s0)


def _lane_row(vec, offset):
    return jnp.zeros((1, LANE), F32).at[0, offset:offset + vec.shape[0]].set(vec.astype(F32))


def kernel(x_prompt, x_sample, cache_k, cache_v, state_conv, state_ssm, page_table, p_prompt, p_sample, norm_pre, norm_post, w_in_a, lambda_q1, lambda_k1, lambda_q2, lambda_k2, subln_a, w_out_a, w_in_b, conv_b, a_log_b, dt_bias_b, onorm_b, w_out_b, w_ple, w_ple_gate):
    bsz, seq, d = x_prompt.shape
    dec_b, dec_seq, _ = x_sample.shape
    depth = norm_pre.shape[0]
    n_phys = cache_k.shape[1]
    n_pages = page_table.shape[1]
    assert d == A_HEADS * A_V_DIM == B_HEADS * B_HEAD_DIM
    assert seq % GDN_ROWS == 0 and GDN_ROWS % GDN_CHUNK == 0

    hp = x_prompt.reshape(bsz * seq, d)
    hs = x_sample.reshape(dec_b * dec_seq, d)
    cache_k2 = cache_k.reshape((cache_k.shape[0] * n_phys,) + cache_k.shape[2:])
    cache_v2 = cache_v.reshape((cache_v.shape[0] * n_phys,) + cache_v.shape[2:])
    zero_conv = jnp.zeros((bsz, CONV_WIDTH - 1, 3 * d), F32)
    zero_ssm = jnp.zeros((bsz, B_HEADS, B_HEAD_DIM, B_HEAD_DIM), F32)
    ssm_all = state_ssm.reshape((state_ssm.shape[0] * dec_b,) + state_ssm.shape[2:])
    pp_all = p_prompt.reshape(depth * bsz * seq, -1)
    ps_all = p_sample.reshape(depth * dec_b * dec_seq, -1)

    n_attn = (depth + N_MIXERS - 1) // N_MIXERS
    pre_gains = norm_pre.reshape(depth, 1, d)
    post_gains = norm_post.reshape(depth, 1, d)
    w_in_a16 = w_in_a.astype(BF16)
    w_out_a16 = w_out_a.astype(BF16)
    w_out_b16 = w_out_b.astype(BF16)
    w_ple16 = w_ple.astype(BF16)
    w_gate16 = w_ple_gate.astype(BF16)

    kv_p, kv_s = None, None
    nc_p, ns_p, nc_s, ns_s = [], [], [], []
    for i in range(depth):
        j = i // N_MIXERS
        if i % N_MIXERS == 0:
            lam_init = 0.8 - 0.6 * math.exp(-0.3 * i)
            lam_refs = [t[j].reshape(1, A_HEAD_DIM) for t in (lambda_q1, lambda_k1, lambda_q2, lambda_k2)]
            sub_gain = jnp.tile(subln_a[j].reshape(1, A_V_DIM), (1, A_HEADS))
            q16, k_p, v_p, k16, v16, gate_p = _proj_attn(hp, pre_gains, i, w_in_a16, j, n_attn, kv_p, BF16)
            kv_p = (k_p, v_p)
            q32, k_s, v_s, _, _, gate_s = _proj_attn(hs, pre_gains, i, w_in_a16, j, n_attn, kv_s, F32)
            kv_s = (k_s, v_s)
            page_rows = (page_table + j * n_phys).reshape(dec_b * n_pages)
            o_p, o_s = _attention(q16, k16, v16, q32, k_s.reshape((-1,) + k_s.shape[2:]),
                                  v_s.reshape((-1,) + v_s.shape[2:]), j * dec_b, cache_k2, cache_v2, page_rows,
                                  lam_refs, bsz, dec_b, lam_init)
            wout16_all, head_dim, o_scale = w_out_a16, A_V_DIM, 1.0 - lam_init
        else:
            w16 = w_in_b[j][:, :4 * d].astype(BF16)
            wba = w_in_b[j][:, 4 * d:]
            wba16 = jnp.zeros((d, LANE), F32).at[:, :2 * B_HEADS].set(wba).astype(BF16)
            sub_gain = jnp.tile(onorm_b[j].reshape(1, B_HEAD_DIM), (1, B_HEADS))
            alog_row = _lane_row(a_log_b[j], B_HEADS)
            dtb_row = _lane_row(dt_bias_b[j], B_HEADS)
            qkv, gate_p, ba = _proj_gdn(hp, pre_gains, i, w16, wba16)
            o_p, cbuf, sst = _gdn(qkv.reshape(bsz, seq, 3 * d), ba.reshape(bsz, seq, LANE), conv_b[j],
                                  alog_row, dtb_row, zero_conv, zero_ssm, 0, GDN_ROWS)
            o_p = o_p.reshape(bsz * seq, d)
            nc_p.append(cbuf)
            ns_p.append(sst)
            qkv, gate_s, ba = _proj_gdn(hs, pre_gains, i, w16, wba16)
            o_s, cbuf, sst = _gdn(qkv.reshape(dec_b, dec_seq, 3 * d), ba.reshape(dec_b, dec_seq, LANE), conv_b[j],
                                  alog_row, dtb_row, state_conv[j], ssm_all, j * dec_b, dec_seq)
            o_s = o_s.reshape(dec_b * dec_seq, d)
            nc_s.append(cbuf)
            ns_s.append(sst)
            wout16_all, head_dim, o_scale = w_out_b16, B_HEAD_DIM, 1.0
        hp = _post(o_p, gate_p, hp, pp_all, i, sub_gain, post_gains, wout16_all, j, w_ple16, w_gate16, head_dim, o_scale)
        hs = _post(o_s, gate_s, hs, ps_all, i, sub_gain, post_gains, wout16_all, j, w_ple16, w_gate16, head_dim, o_scale)

    def per_seq(stacked, n):
        return stacked.reshape((stacked.shape[0], n, -1) + stacked.shape[2:])

    return (hp.reshape(bsz, seq, d), hs.reshape(dec_b, dec_seq, d),
            per_seq(kv_p[0], bsz), per_seq(kv_p[1], bsz), jnp.stack(nc_p), jnp.stack(ns_p),
            per_seq(kv_s[0], dec_b), per_seq(kv_s[1], dec_b), jnp.stack(nc_s), jnp.stack(ns_s))
```
